```python
import jax, jax.numpy as jnp
from jax import lax
import numpy as np

D_MODEL = 1024
BATCH = 16
SEQ = 2048
DEPTH = 1

N_MEM = 256
EPS = 1e-6
POOL_WINDOWS = (2, 4, 8, 16)
N_POOL_GROUPS = len(POOL_WINDOWS)
POOL_WIDTH = D_MODEL // 2
POOL_GC = POOL_WIDTH // N_POOL_GROUPS
FOX_HEADS = 8
FOX_DH = 64
FOX_WIDTH = FOX_HEADS * FOX_DH
Q_BLOCK = 128
GATE_WIDTH = 2 * D_MODEL
IN_SPLITS = (POOL_WIDTH, POOL_WIDTH + FOX_WIDTH, POOL_WIDTH + 2 * FOX_WIDTH,
             POOL_WIDTH + 3 * FOX_WIDTH, POOL_WIDTH + 3 * FOX_WIDTH + FOX_HEADS)
IN_COLS = POOL_WIDTH + 3 * FOX_WIDTH + FOX_HEADS + GATE_WIDTH
X_HEADS = 4
X_DH = 128
X_WIDTH = X_HEADS * X_DH
D_FF = ((8 * D_MODEL // 3 + 255) // 256) * 256

kernel_name = "hybrid_pool_fox_gated_block"


def rmsnorm(x, g):
    xf = x.astype(jnp.float32)
    y = xf * lax.rsqrt(jnp.mean(xf * xf, axis=-1, keepdims=True) + EPS)
    return (y * g.astype(jnp.float32)).astype(x.dtype)


def pool_mixer(u, pool_w, pool_scale):
    B, S, _ = u.shape
    uf = u.astype(jnp.float32)
    cs = jnp.cumsum(uf, axis=1)
    t = jnp.arange(S)
    outs = []
    for g, w in enumerate(POOL_WINDOWS):
        sl = slice(g * POOL_GC, (g + 1) * POOL_GC)
        cs_g = cs[..., sl]
        shifted = jnp.pad(cs_g, ((0, 0), (w, 0), (0, 0)))[:, :S]
        cnt = jnp.minimum(t + 1, w).astype(jnp.float32)[None, :, None]
        outs.append((cs_g - shifted) / cnt - uf[..., sl])
    d = jnp.stack(outs, axis=2).astype(u.dtype)
    y = jnp.einsum('bsgc,gcd->bsgd', d, pool_w).reshape(B, S, POOL_WIDTH)
    return y * pool_scale


def forgetting_attention(q, k, v, f_logit):
    B, S, H, dh = q.shape
    q = q.transpose(0, 2, 1, 3)
    k = k.transpose(0, 2, 1, 3)
    v = v.transpose(0, 2, 1, 3)
    c = jnp.cumsum(jax.nn.log_sigmoid(f_logit.astype(jnp.float32)), axis=1).transpose(0, 2, 1)
    scale = dh ** -0.5
    outs = []
    for i in range(S // Q_BLOCK):
        q0, q1 = i * Q_BLOCK, (i + 1) * Q_BLOCK
        qb = q[:, :, q0:q1]
        kp, vp, cp = k[:, :, :q1], v[:, :, :q1], c[:, :, :q1]
        s = jnp.einsum('bhqd,bhkd->bhqk', qb, kp).astype(jnp.float32) * scale
        s = s + c[:, :, q0:q1, None] - cp[:, :, None, :]
        mask = (q0 + jnp.arange(Q_BLOCK))[:, None] >= jnp.arange(q1)[None, :]
        s = jnp.where(mask[None, None], s, -1e30)
        p = jax.nn.softmax(s, axis=-1).astype(v.dtype)
        outs.append(jnp.einsum('bhqk,bhkd->bhqd', p, vp))
    o = jnp.concatenate(outs, axis=2)
    return o.transpose(0, 2, 1, 3).reshape(B, S, H * dh)


def memory_cross_attention(h, mem_n, w_xq, w_xkv, w_xo):
    B, S, _ = h.shape
    M = mem_n.shape[1]
    q = (h @ w_xq).reshape(B, S, X_HEADS, X_DH)
    kv = (mem_n @ w_xkv).reshape(B, M, 2, X_HEADS, X_DH)
    k, v = kv[:, :, 0], kv[:, :, 1]
    s = jnp.einsum('bshd,bmhd->bhsm', q, k).astype(jnp.float32) * (X_DH ** -0.5)
    p = jax.nn.softmax(s, axis=-1).astype(v.dtype)
    o = jnp.einsum('bhsm,bmhd->bshd', p, v).reshape(B, S, X_WIDTH)
    return o @ w_xo


def setup_inputs(seed: int = 0) -> dict:
    key = jax.random.key(seed)
    ks = jax.random.split(key, 24)
    f32 = jnp.float32

    def nrm(k, shape, fan_in):
        return jax.random.normal(k, shape, f32) * (fan_in ** -0.5)

    def gain(k, shape):
        return 1.0 + 0.05 * jax.random.normal(k, shape, f32)

    L = DEPTH
    return {
        "x": jax.random.normal(ks[0], (BATCH, SEQ, D_MODEL), f32),
        "mem": jax.random.normal(ks[1], (BATCH, N_MEM, D_MODEL), f32),
        "norm_mix_g": gain(ks[2], (L, D_MODEL)),
        "w_in": nrm(ks[3], (L, D_MODEL, IN_COLS), D_MODEL),
        "b_forget": 3.0 + 0.5 * jax.random.normal(ks[4], (L, FOX_HEADS), f32),
        "b_gate": 0.02 * jax.random.normal(ks[5], (L, GATE_WIDTH), f32),
        "pool_w": nrm(ks[6], (L, N_POOL_GROUPS, POOL_GC, POOL_GC), POOL_GC),
        "pool_scale": gain(ks[7], (L, POOL_WIDTH)),
        "w_pool_out": nrm(ks[8], (L, POOL_WIDTH, D_MODEL), POOL_WIDTH),
        "w_fox_out": nrm(ks[9], (L, FOX_WIDTH, D_MODEL), FOX_WIDTH),
        "w_out": nrm(ks[10], (L, D_MODEL, D_MODEL), D_MODEL),
        "norm_x_g": gain(ks[11], (L, D_MODEL)),
        "norm_mem_g": gain(ks[12], (L, D_MODEL)),
        "w_xq": nrm(ks[13], (L, D_MODEL, X_WIDTH), D_MODEL),
        "w_xkv": nrm(ks[14], (L, D_MODEL, 2 * X_WIDTH), D_MODEL),
        "w_xo": nrm(ks[15], (L, X_WIDTH, D_MODEL), X_WIDTH),
        "norm_ffn_g": gain(ks[16], (L, D_MODEL)),
        "w_ffn_in": nrm(ks[17], (L, D_MODEL, 2 * D_FF), D_MODEL),
        "w_ffn_out": nrm(ks[18], (L, D_FF, D_MODEL), D_FF),
        "norm_final_g": gain(ks[19], (D_MODEL,)),
    }


def reference(x, mem, norm_mix_g, w_in, b_forget, b_gate, pool_w, pool_scale,
              w_pool_out, w_fox_out, w_out, norm_x_g, norm_mem_g, w_xq, w_xkv, w_xo,
              norm_ffn_g, w_ffn_in, w_ffn_out, norm_final_g):
    B, S, D = x.shape
    for l in range(DEPTH):
        h = rmsnorm(x, norm_mix_g[l])
        proj = h @ w_in[l]
        u_pool, q, k, v, f_logit, g_logit = jnp.split(proj, IN_SPLITS, axis=-1)
        y_pool = pool_mixer(u_pool, pool_w[l], pool_scale[l]) @ w_pool_out[l]
        y_fox = forgetting_attention(q.reshape(B, S, FOX_HEADS, FOX_DH),
                                     k.reshape(B, S, FOX_HEADS, FOX_DH),
                                     v.reshape(B, S, FOX_HEADS, FOX_DH),
                                     f_logit + b_forget[l]) @ w_fox_out[l]
        gates = jax.nn.sigmoid(g_logit + b_gate[l])
        g_pool, g_fox = gates[..., :D], gates[..., D:]
        x = x + (g_pool * y_pool + g_fox * y_fox) @ w_out[l]
        hx = rmsnorm(x, norm_x_g[l])
        mem_n = rmsnorm(mem, norm_mem_g[l])
        x = x + memory_cross_attention(hx, mem_n, w_xq[l], w_xkv[l], w_xo[l])
        hf = rmsnorm(x, norm_ffn_g[l])
        gu = hf @ w_ffn_in[l]
        gt, up = gu[..., :D_FF], gu[..., D_FF:]
        x = x + (jax.nn.silu(gt) * up) @ w_ffn_out[l]
    return rmsnorm(x, norm_final_g)
```

```python
import functools

import jax
import jax.numpy as jnp
from jax import lax
from jax.experimental import pallas as pl
from jax.experimental.pallas import tpu as pltpu

F32 = jnp.float32
BF16 = jnp.bfloat16

EPS = 1e-6
POOL_WINDOWS = (2, 4, 8, 16)
POOL_GC = 128
POOL_WIDTH = 512
MAX_WINDOW = 16
FOX_HEADS = 8
FOX_DH = 64
FOX_WIDTH = 512
X_HEADS = 4
X_DH = 128
X_WIDTH = 512
LANES = 128
VMEM_LIMIT = 56 * 1024 * 1024
NEG_BIG = -1e30

ROW_TILE = 512
ATT_TILE = 256


def _rmsnorm(x, g):
    return x * lax.rsqrt(jnp.mean(x * x, axis=-1, keepdims=True) + EPS) * g


def _log_sigmoid(x):
    return jnp.minimum(x, 0.0) - jnp.log1p(jnp.exp(-jnp.abs(x)))


def _sigmoid(x):
    return 1.0 / (1.0 + jnp.exp(-x))


def _const_spec(shape):
    return pl.BlockSpec(shape, lambda *_: (0,) * len(shape))


def _params(*sem):
    return pltpu.CompilerParams(dimension_semantics=sem, vmem_limit_bytes=VMEM_LIMIT)


def _mem_kv_kernel(mem_ref, g_ref, w_ref, kv_ref):
    mn = _rmsnorm(mem_ref[0], g_ref[...]).astype(BF16)
    kv_ref[0] = jnp.dot(mn, w_ref[...], preferred_element_type=F32).astype(BF16)


def _mem_kv(mem, g, w_xkv):
    B, M, D = mem.shape
    return pl.pallas_call(
        _mem_kv_kernel,
        grid=(B,),
        in_specs=[pl.BlockSpec((1, M, D), lambda b: (b, 0, 0)),
                  _const_spec((1, D)), _const_spec(w_xkv.shape)],
        out_specs=pl.BlockSpec((1, M, 2 * X_WIDTH), lambda b: (b, 0, 0)),
        out_shape=jax.ShapeDtypeStruct((B, M, 2 * X_WIDTH), BF16),
        compiler_params=_params("arbitrary"),
        name="mem_kv",
    )(mem, g, w_xkv)


def _in_proj_kernel(x_ref, g_ref, wu_ref, wqkv_ref, wf_ref, bf_ref,
                    u_ref, qkv_ref, c_ref, csum_ref):
    tm = x_ref.shape[1]

    @pl.when(pl.program_id(1) == 0)
    def _():
        csum_ref[...] = jnp.zeros_like(csum_ref)

    h = _rmsnorm(x_ref[0], g_ref[...]).astype(BF16)
    u_ref[0] = jnp.dot(h, wu_ref[...], preferred_element_type=F32)
    qkv = jnp.dot(h, wqkv_ref[...], preferred_element_type=F32)
    qkv_ref[0, :, :FOX_WIDTH] = (qkv[:, :FOX_WIDTH] * (FOX_DH ** -0.5)).astype(BF16)
    qkv_ref[0, :, FOX_WIDTH:] = qkv[:, FOX_WIDTH:].astype(BF16)

    logf = _log_sigmoid(jnp.dot(h, wf_ref[...], preferred_element_type=F32) + bf_ref[...])
    p0 = logf.astype(BF16)
    r1 = logf - p0.astype(F32)
    p1 = r1.astype(BF16)
    p2 = (r1 - p1.astype(F32)).astype(BF16)
    row = lax.broadcasted_iota(jnp.int32, (tm, tm), 0)
    col = lax.broadcasted_iota(jnp.int32, (tm, tm), 1)
    tri = jnp.where(row >= col, 1.0, 0.0).astype(BF16)
    parts = jnp.dot(tri, jnp.concatenate([p0, p1, p2], axis=1), preferred_element_type=F32)
    c = parts[:, :LANES] + parts[:, LANES:2 * LANES] + parts[:, 2 * LANES:] + csum_ref[...]
    c_ref[0] = c
    csum_ref[...] = c[tm - 1:tm, :]


def _in_proj(x, g, w_u, w_qkv, w_f, b_f, tm):
    B, S, D = x.shape
    row = lambda n: pl.BlockSpec((1, tm, n), lambda b, s: (b, s, 0))
    return pl.pallas_call(
        _in_proj_kernel,
        grid=(B, S // tm),
        in_specs=[row(D), _const_spec((1, D)), _const_spec(w_u.shape), _const_spec(w_qkv.shape),
                  _const_spec(w_f.shape), _const_spec((1, LANES))],
        out_specs=[row(POOL_WIDTH), row(3 * FOX_WIDTH), row(LANES)],
        out_shape=[jax.ShapeDtypeStruct((B, S, POOL_WIDTH), F32),
                   jax.ShapeDtypeStruct((B, S, 3 * FOX_WIDTH), BF16),
                   jax.ShapeDtypeStruct((B, S, LANES), F32)],
        scratch_shapes=[pltpu.VMEM((1, LANES), F32)],
        compiler_params=_params("arbitrary", "arbitrary"),
        name="in_proj",
    )(x, g, w_u, w_qkv, w_f, b_f)


def _fox_kernel(q_ref, k_ref, v_ref, ccol_ref, crow_ref, o_ref):
    t = q_ref.shape[1]
    pair = pl.program_id(1)
    qi = pl.program_id(2)
    q = q_ref[0]
    ccol = ccol_ref[0]
    lane = lax.broadcasted_iota(jnp.int32, (1, LANES), 1)
    first = lane < FOX_DH
    causal = (lax.broadcasted_iota(jnp.int32, (t, t), 0) >=
              lax.broadcasted_iota(jnp.int32, (t, t), 1))
    zero = jnp.zeros_like(q)

    outs = []
    for hh in range(2):
        head = 2 * pair + hh
        qm = jnp.where(first if hh == 0 else jnp.logical_not(first), q, zero)
        cq = jnp.sum(jnp.where(lane == head, ccol, 0.0), axis=1, keepdims=True)

        def block(j, carry, masked):
            m, l, acc = carry
            kb = k_ref[0, pl.ds(pl.multiple_of(j * t, t), t), :]
            vb = v_ref[0, pl.ds(pl.multiple_of(j * t, t), t), :]
            ck = crow_ref[0, head, pl.ds(j, 1), :]
            s = lax.dot_general(qm, kb, (((1,), (1,)), ((), ())), preferred_element_type=F32)
            s = s + cq - ck
            if masked:
                s = jnp.where(causal, s, NEG_BIG)
            m_new = jnp.maximum(m, jnp.max(s, axis=1, keepdims=True))
            alpha = jnp.exp(m - m_new)
            p = jnp.exp(s - m_new)
            l = alpha * l + jnp.sum(p, axis=1, keepdims=True)
            acc = alpha * acc + jnp.dot(p.astype(BF16), vb, preferred_element_type=F32)
            return m_new, l, acc

        init = (jnp.full((t, 1), NEG_BIG, F32), jnp.zeros((t, 1), F32), jnp.zeros((t, LANES), F32))
        carry = lax.fori_loop(0, qi, functools.partial(block, masked=False), init)
        _, l, acc = block(qi, carry, masked=True)
        outs.append(acc / l)
    o_ref[0] = jnp.where(first, outs[0], outs[1]).astype(BF16)


def _fox_attention(qkv, c_col, c_row, t):
    B, S, _ = qkv.shape
    n_pairs = FOX_HEADS // 2
    return pl.pallas_call(
        _fox_kernel,
        grid=(B, n_pairs, S // t),
        in_specs=[pl.BlockSpec((1, t, LANES), lambda b, p, i: (b, i, p)),
                  pl.BlockSpec((1, S, LANES), lambda b, p, i: (b, 0, n_pairs + p)),
                  pl.BlockSpec((1, S, LANES), lambda b, p, i: (b, 0, 2 * n_pairs + p)),
                  pl.BlockSpec((1, t, LANES), lambda b, p, i: (b, i, 0)),
                  pl.BlockSpec((1, FOX_HEADS, S // t, t), lambda b, p, i: (b, 0, 0, 0))],
        out_specs=pl.BlockSpec((1, t, LANES), lambda b, p, i: (b, i, p)),
        out_shape=jax.ShapeDtypeStruct((B, S, FOX_WIDTH), BF16),
        compiler_params=_params("arbitrary", "arbitrary", "arbitrary"),
        name="fox_attn",
    )(qkv, qkv, qkv, c_col, c_row)


def _post_kernel(x_ref, u_ref, o_ref, kv_ref, gmix_ref, wg_ref, bg_ref, pw_ref, ps_ref, wpo_ref,
                 wfo_ref, wo_ref, gx_ref, wxq_ref, wxo_ref, x2_ref, ubuf_ref):
    tm = x_ref.shape[1]
    D = x_ref.shape[2]
    si = pl.program_id(1)
    x = x_ref[0]

    @pl.when(si == 0)
    def _():
        ubuf_ref[0:MAX_WINDOW, :] = jnp.zeros((MAX_WINDOW, POOL_WIDTH), F32)

    @pl.when(si > 0)
    def _():
        ubuf_ref[0:MAX_WINDOW, :] = ubuf_ref[tm:tm + MAX_WINDOW, :]

    u = u_ref[0]
    ubuf_ref[MAX_WINDOW:, :] = u
    pos = si * tm + lax.broadcasted_iota(jnp.int32, (tm, 1), 0)
    ys = []
    for g, w in enumerate(POOL_WINDOWS):
        sl = slice(g * POOL_GC, (g + 1) * POOL_GC)
        ug = u[:, sl]
        ws = ug
        for j in range(1, w):
            ws = ws + ubuf_ref[MAX_WINDOW - j:MAX_WINDOW - j + tm, sl]
        cnt = jnp.minimum(pos + 1, w).astype(F32)
        d = (ws / cnt - ug).astype(BF16)
        ys.append(jnp.dot(d, pw_ref[g], preferred_element_type=F32))
    y = (jnp.concatenate(ys, axis=1) * ps_ref[...]).astype(BF16)
    y_pool = jnp.dot(y, wpo_ref[...], preferred_element_type=F32)

    h = _rmsnorm(x, gmix_ref[...]).astype(BF16)
    gates = _sigmoid(jnp.dot(h, wg_ref[...], preferred_element_type=F32) + bg_ref[...])
    y_fox = jnp.dot(o_ref[0], wfo_ref[...], preferred_element_type=F32)
    mix = (gates[:, :D] * y_pool + gates[:, D:] * y_fox).astype(BF16)
    x1 = x + jnp.dot(mix, wo_ref[...], preferred_element_type=F32)

    hx = _rmsnorm(x1, gx_ref[...]).astype(BF16)
    qx = jnp.dot(hx, wxq_ref[...], preferred_element_type=F32).astype(BF16)
    kv = kv_ref[0]
    heads = []
    for hd in range(X_HEADS):
        sl = slice(hd * X_DH, (hd + 1) * X_DH)
        s = lax.dot_general(qx[:, sl], kv[:, sl], (((1,), (1,)), ((), ())),
                            preferred_element_type=F32) * (X_DH ** -0.5)
        p = jnp.exp(s - jnp.max(s, axis=1, keepdims=True))
        l = jnp.sum(p, axis=1, keepdims=True)
        vh = kv[:, X_WIDTH + hd * X_DH:X_WIDTH + (hd + 1) * X_DH]
        heads.append(jnp.dot(p.astype(BF16), vh, preferred_element_type=F32) / l)
    ox = jnp.concatenate(heads, axis=1).astype(BF16)
    x2_ref[0] = x1 + jnp.dot(ox, wxo_ref[...], preferred_element_type=F32)


def _post(x, u, o, kv, gmix, w_g, b_g, pool_w, pool_scale, w_po, w_fo, w_o, gx, w_xq, w_xo, tm):
    B, S, D = x.shape
    M = kv.shape[1]
    row = lambda n: pl.BlockSpec((1, tm, n), lambda b, s: (b, s, 0))
    consts = [gmix, w_g, b_g, pool_w, pool_scale, w_po, w_fo, w_o, gx, w_xq, w_xo]
    return pl.pallas_call(
        _post_kernel,
        grid=(B, S // tm),
        in_specs=[row(D), row(POOL_WIDTH), row(FOX_WIDTH),
                  pl.BlockSpec((1, M, 2 * X_WIDTH), lambda b, s: (b, 0, 0))]
                 + [_const_spec(a.shape) for a in consts],
        out_specs=row(D),
        out_shape=jax.ShapeDtypeStruct((B, S, D), F32),
        scratch_shapes=[pltpu.VMEM((tm + MAX_WINDOW, POOL_WIDTH), F32)],
        compiler_params=_params("arbitrary", "arbitrary"),
        name="post",
    )(x, u, o, kv, *consts)


def _ffn_kernel(x_ref, g_ref, wi_ref, wo_ref, gfin_ref, out_ref, *, final_norm):
    d_ff = wo_ref.shape[0]
    x = x_ref[...]
    hf = _rmsnorm(x, g_ref[...]).astype(BF16)
    gu = jnp.dot(hf, wi_ref[...], preferred_element_type=F32)
    gt = gu[:, :d_ff]
    act = (gt * _sigmoid(gt) * gu[:, d_ff:]).astype(BF16)
    x3 = x + jnp.dot(act, wo_ref[...], preferred_element_type=F32)
    out_ref[...] = _rmsnorm(x3, gfin_ref[...]) if final_norm else x3


def _ffn(x2, g, w_i, w_o, g_fin, tm, final_norm):
    T, D = x2.shape
    return pl.pallas_call(
        functools.partial(_ffn_kernel, final_norm=final_norm),
        grid=(T // tm,),
        in_specs=[pl.BlockSpec((tm, D), lambda i: (i, 0)), _const_spec((1, D)),
                  pl.BlockSpec(w_i.shape, lambda i: (0, 0), pipeline_mode=pl.Buffered(1)),
                  pl.BlockSpec(w_o.shape, lambda i: (0, 0), pipeline_mode=pl.Buffered(1)),
                  _const_spec((1, D))],
        out_specs=pl.BlockSpec((tm, D), lambda i: (i, 0)),
        out_shape=jax.ShapeDtypeStruct((T, D), F32),
        compiler_params=_params("arbitrary"),
        name="ffn",
    )(x2, g, w_i, w_o, g_fin)


def kernel(x, mem, norm_mix_g, w_in, b_forget, b_gate, pool_w, pool_scale, w_pool_out, w_fox_out,
           w_out, norm_x_g, norm_mem_g, w_xq, w_xkv, w_xo, norm_ffn_g, w_ffn_in, w_ffn_out,
           norm_final_g):
    B, S, D = x.shape
    depth = w_in.shape[0]
    qkv_end = POOL_WIDTH + 3 * FOX_WIDTH
    f_end = qkv_end + FOX_HEADS
    row2 = lambda a: a.reshape(1, -1)
    for l in range(depth):
        wl = w_in[l].astype(BF16)
        w_f = jnp.pad(wl[:, qkv_end:f_end], ((0, 0), (0, LANES - FOX_HEADS)))
        b_f = jnp.pad(b_forget[l], (0, LANES - FOX_HEADS)).reshape(1, LANES)
        kv = _mem_kv(mem, row2(norm_mem_g[l]), w_xkv[l].astype(BF16))
        u, qkv, c = _in_proj(x, row2(norm_mix_g[l]), wl[:, :POOL_WIDTH], wl[:, POOL_WIDTH:qkv_end],
                             w_f, b_f, ROW_TILE)
        c_row = jnp.swapaxes(c[:, :, :FOX_HEADS], 1, 2).reshape(B, FOX_HEADS, S // ATT_TILE, ATT_TILE)
        o = _fox_attention(qkv, c, c_row, ATT_TILE)
        x2 = _post(x, u, o, kv, row2(norm_mix_g[l]), wl[:, f_end:], row2(b_gate[l]),
                   pool_w[l].astype(BF16), row2(pool_scale[l]), w_pool_out[l].astype(BF16),
                   w_fox_out[l].astype(BF16), w_out[l].astype(BF16), row2(norm_x_g[l]),
                   w_xq[l].astype(BF16), w_xo[l].astype(BF16), ROW_TILE)
        x = _ffn(x2.reshape(B * S, D), row2(norm_ffn_g[l]), w_ffn_in[l].astype(BF16),
                 w_ffn_out[l].astype(BF16), row2(norm_final_g), ROW_TILE,
                 final_norm=(l == depth - 1)).reshape(B, S, D)
    return x
```

```python
import functools

import jax
import jax.numpy as jnp
from jax import lax
from jax.experimental import pallas as pl
from jax.experimental.pallas import tpu as pltpu

F32 = jnp.float32
BF16 = jnp.bfloat16

EPS = 1e-6
POOL_WINDOWS = (2, 4, 8, 16)
POOL_GC = 128
POOL_WIDTH = 512
MAX_WINDOW = 16
FOX_HEADS = 8
FOX_DH = 64
FOX_WIDTH = 512
X_HEADS = 4
X_DH = 128
X_WIDTH = 512
LANES = 128
VMEM_LIMIT = 56 * 1024 * 1024
NEG_BIG = -1e30
AUG_TERMS = 3

ROW_TILE = 512
ATT_TILE = 256


def _rmsnorm(x, g):
    return x * lax.rsqrt(jnp.mean(x * x, axis=-1, keepdims=True) + EPS) * g


def _log_sigmoid(x):
    return jnp.minimum(x, 0.0) - jnp.log1p(jnp.exp(-jnp.abs(x)))


def _sigmoid(x):
    return 1.0 / (1.0 + jnp.exp(-x))


def _const_spec(shape):
    return pl.BlockSpec(shape, lambda *_: (0,) * len(shape))


def _params(*sem):
    return pltpu.CompilerParams(dimension_semantics=sem, vmem_limit_bytes=VMEM_LIMIT)


def _mem_kv_kernel(mem_ref, g_ref, w_ref, kv_ref):
    mn = _rmsnorm(mem_ref[0], g_ref[...]).astype(BF16)
    kv_ref[0] = jnp.dot(mn, w_ref[...], preferred_element_type=F32).astype(BF16)


def _mem_kv(mem, g, w_xkv):
    B, M, D = mem.shape
    return pl.pallas_call(
        _mem_kv_kernel,
        grid=(B,),
        in_specs=[pl.BlockSpec((1, M, D), lambda b: (b, 0, 0)),
                  _const_spec((1, D)), _const_spec(w_xkv.shape)],
        out_specs=pl.BlockSpec((1, M, 2 * X_WIDTH), lambda b: (b, 0, 0)),
        out_shape=jax.ShapeDtypeStruct((B, M, 2 * X_WIDTH), BF16),
        compiler_params=_params("arbitrary"),
        name="mem_kv",
    )(mem, g, w_xkv)


def _split3(c):
    hi = c.astype(BF16).astype(F32)
    r = c - hi
    lo = r.astype(BF16).astype(F32)
    return hi, lo, r - lo


def _augment(feat, c_col, in_aug, a, query):
    pieces = _split3(c_col)
    ones_lo = AUG_TERMS if query else 0
    aug = jnp.where((a >= ones_lo) & (a < ones_lo + AUG_TERMS), 1.0, 0.0)
    for i, piece in enumerate(pieces):
        aug = jnp.where(a == (i if query else AUG_TERMS + i), piece if query else -piece, aug)
    return jnp.where(in_aug, aug, feat).astype(BF16)


def _in_proj_kernel(x_ref, g_ref, wu_ref, wqkv_ref, wf_ref, bf_ref,
                    u_ref, qa_ref, ka_ref, v_ref, csum_ref):
    tm = x_ref.shape[1]

    @pl.when(pl.program_id(1) == 0)
    def _():
        csum_ref[...] = jnp.zeros_like(csum_ref)

    h = _rmsnorm(x_ref[0], g_ref[...]).astype(BF16)
    u_ref[0] = jnp.dot(h, wu_ref[...], preferred_element_type=F32)
    qkv = jnp.dot(h, wqkv_ref[...], preferred_element_type=F32)
    v_ref[0] = qkv[:, 2 * FOX_WIDTH:].astype(BF16)

    logf = _log_sigmoid(jnp.dot(h, wf_ref[...], preferred_element_type=F32) + bf_ref[...])
    row = lax.broadcasted_iota(jnp.int32, (tm, tm), 0)
    col = lax.broadcasted_iota(jnp.int32, (tm, tm), 1)
    tri = jnp.where(row >= col, 1.0, 0.0).astype(BF16)
    parts = jnp.dot(tri, jnp.concatenate(_split3(logf), axis=1).astype(BF16),
                    preferred_element_type=F32)
    c = parts[:, :LANES] + parts[:, LANES:2 * LANES] + parts[:, 2 * LANES:] + csum_ref[...]
    csum_ref[...] = c[tm - 1:tm, :]

    lane = lax.broadcasted_iota(jnp.int32, (1, LANES), 1)
    for pair in range(FOX_HEADS // 2):
        qp = qkv[:, pair * LANES:(pair + 1) * LANES] * (FOX_DH ** -0.5)
        kp = qkv[:, FOX_WIDTH + pair * LANES:FOX_WIDTH + (pair + 1) * LANES]
        for hh in range(2):
            head = 2 * pair + hh
            in_aug = (lane >= FOX_DH) if hh == 0 else (lane < FOX_DH)
            a = (lane - FOX_DH) if hh == 0 else lane
            c_col = c[:, head:head + 1]
            hs = slice(head * LANES, (head + 1) * LANES)
            qa_ref[0, :, hs] = _augment(qp, c_col, in_aug, a, query=True)
            ka_ref[0, :, hs] = _augment(kp, c_col, in_aug, a, query=False)


def _in_proj(x, g, w_u, w_qkv, w_f, b_f, tm):
    B, S, D = x.shape
    row = lambda n: pl.BlockSpec((1, tm, n), lambda b, s: (b, s, 0))
    return pl.pallas_call(
        _in_proj_kernel,
        grid=(B, S // tm),
        in_specs=[row(D), _const_spec((1, D)), _const_spec(w_u.shape), _const_spec(w_qkv.shape),
                  _const_spec(w_f.shape), _const_spec((1, LANES))],
        out_specs=[row(POOL_WIDTH), row(FOX_HEADS * LANES), row(FOX_HEADS * LANES), row(FOX_WIDTH)],
        out_shape=[jax.ShapeDtypeStruct((B, S, POOL_WIDTH), F32),
                   jax.ShapeDtypeStruct((B, S, FOX_HEADS * LANES), BF16),
                   jax.ShapeDtypeStruct((B, S, FOX_HEADS * LANES), BF16),
                   jax.ShapeDtypeStruct((B, S, FOX_WIDTH), BF16)],
        scratch_shapes=[pltpu.VMEM((1, LANES), F32)],
        compiler_params=_params("arbitrary", "arbitrary"),
        name="in_proj",
    )(x, g, w_u, w_qkv, w_f, b_f)


def _fox_kernel(q_ref, k_ref, v_ref, o_ref):
    t = q_ref.shape[1]
    qi = pl.program_id(1)
    n_pairs = FOX_HEADS // 2
    first = lax.broadcasted_iota(jnp.int32, (1, LANES), 1) < FOX_DH
    causal = (lax.broadcasted_iota(jnp.int32, (t, t), 0) >=
              lax.broadcasted_iota(jnp.int32, (t, t), 1))
    nt_dims = (((1,), (1,)), ((), ()))

    def block(j, carry, masked):
        ms, ls, accs = carry
        rows = pl.ds(pl.multiple_of(j * t, t), t)
        new_m, new_l, new_acc = [], [], []
        for pr in range(n_pairs):
            vb = v_ref[0, rows, pr * LANES:(pr + 1) * LANES]
            alphas, pvs = [], []
            for hh in range(2):
                hd = 2 * pr + hh
                hs = slice(hd * LANES, (hd + 1) * LANES)
                s = lax.dot_general(q_ref[0, :, hs], k_ref[0, rows, hs], nt_dims,
                                    preferred_element_type=F32)
                if masked:
                    s = jnp.where(causal, s, NEG_BIG)
                m_new = jnp.maximum(ms[hd], jnp.max(s, axis=1, keepdims=True))
                alpha = jnp.exp(ms[hd] - m_new)
                p = jnp.exp(s - m_new)
                new_m.append(m_new)
                new_l.append(alpha * ls[hd] + jnp.sum(p, axis=1, keepdims=True))
                alphas.append(alpha)
                pvs.append(jnp.dot(p.astype(BF16), vb, preferred_element_type=F32))
            new_acc.append(jnp.where(first, alphas[0], alphas[1]) * accs[pr]
                           + jnp.where(first, pvs[0], pvs[1]))
        return tuple(new_m), tuple(new_l), tuple(new_acc)

    init = (tuple(jnp.full((t, 1), NEG_BIG, F32) for _ in range(FOX_HEADS)),
            tuple(jnp.zeros((t, 1), F32) for _ in range(FOX_HEADS)),
            tuple(jnp.zeros((t, LANES), F32) for _ in range(n_pairs)))
    carry = lax.fori_loop(0, qi, functools.partial(block, masked=False), init)
    _, ls, accs = block(qi, carry, masked=True)
    for pr in range(n_pairs):
        l_pair = jnp.where(first, ls[2 * pr], ls[2 * pr + 1])
        o_ref[0, :, pr * LANES:(pr + 1) * LANES] = (accs[pr] / l_pair).astype(BF16)


def _fox_attention(qa, ka, v, t):
    B, S, _ = v.shape
    return pl.pallas_call(
        _fox_kernel,
        grid=(B, S // t),
        in_specs=[pl.BlockSpec((1, t, FOX_HEADS * LANES), lambda b, i: (b, i, 0)),
                  pl.BlockSpec((1, S, FOX_HEADS * LANES), lambda b, i: (b, 0, 0)),
                  pl.BlockSpec((1, S, FOX_WIDTH), lambda b, i: (b, 0, 0))],
        out_specs=pl.BlockSpec((1, t, FOX_WIDTH), lambda b, i: (b, i, 0)),
        out_shape=jax.ShapeDtypeStruct((B, S, FOX_WIDTH), BF16),
        compiler_params=_params("arbitrary", "arbitrary"),
        name="fox_attn",
    )(qa, ka, v)


def _post_kernel(x_ref, u_ref, o_ref, kv_ref, gmix_ref, wg_ref, bg_ref, pw_ref, ps_ref, wpo_ref,
                 wfo_ref, wo_ref, gx_ref, wxq_ref, wxo_ref, x2_ref, ubuf_ref):
    tm = x_ref.shape[1]
    D = x_ref.shape[2]
    si = pl.program_id(1)
    x = x_ref[0]

    @pl.when(si == 0)
    def _():
        ubuf_ref[0:MAX_WINDOW, :] = jnp.zeros((MAX_WINDOW, POOL_WIDTH), F32)

    @pl.when(si > 0)
    def _():
        ubuf_ref[0:MAX_WINDOW, :] = ubuf_ref[tm:tm + MAX_WINDOW, :]

    u = u_ref[0]
    ubuf_ref[MAX_WINDOW:, :] = u
    pos = si * tm + lax.broadcasted_iota(jnp.int32, (tm, 1), 0)
    ys = []
    for g, w in enumerate(POOL_WINDOWS):
        sl = slice(g * POOL_GC, (g + 1) * POOL_GC)
        ug = u[:, sl]
        ws = ug
        for j in range(1, w):
            ws = ws + ubuf_ref[MAX_WINDOW - j:MAX_WINDOW - j + tm, sl]
        cnt = jnp.minimum(pos + 1, w).astype(F32)
        d = (ws / cnt - ug).astype(BF16)
        ys.append(jnp.dot(d, pw_ref[g], preferred_element_type=F32))
    y = (jnp.concatenate(ys, axis=1) * ps_ref[...]).astype(BF16)
    y_pool = jnp.dot(y, wpo_ref[...], preferred_element_type=F32)

    h = _rmsnorm(x, gmix_ref[...]).astype(BF16)
    gates = _sigmoid(jnp.dot(h, wg_ref[...], preferred_element_type=F32) + bg_ref[...])
    y_fox = jnp.dot(o_ref[0], wfo_ref[...], preferred_element_type=F32)
    mix = (gates[:, :D] * y_pool + gates[:, D:] * y_fox).astype(BF16)
    x1 = x + jnp.dot(mix, wo_ref[...], preferred_element_type=F32)

    hx = _rmsnorm(x1, gx_ref[...]).astype(BF16)
    qx = jnp.dot(hx, wxq_ref[...], preferred_element_type=F32).astype(BF16)
    kv = kv_ref[0]
    heads = []
    for hd in range(X_HEADS):
        sl = slice(hd * X_DH, (hd + 1) * X_DH)
        s = lax.dot_general(qx[:, sl], kv[:, sl], (((1,), (1,)), ((), ())),
                            preferred_element_type=F32) * (X_DH ** -0.5)
        p = jnp.exp(s - jnp.max(s, axis=1, keepdims=True))
        l = jnp.sum(p, axis=1, keepdims=True)
        vh = kv[:, X_WIDTH + hd * X_DH:X_WIDTH + (hd + 1) * X_DH]
        heads.append(jnp.dot(p.astype(BF16), vh, preferred_element_type=F32) / l)
    ox = jnp.concatenate(heads, axis=1).astype(BF16)
    x2_ref[0] = x1 + jnp.dot(ox, wxo_ref[...], preferred_element_type=F32)


def _post(x, u, o, kv, gmix, w_g, b_g, pool_w, pool_scale, w_po, w_fo, w_o, gx, w_xq, w_xo, tm):
    B, S, D = x.shape
    M = kv.shape[1]
    row = lambda n: pl.BlockSpec((1, tm, n), lambda b, s: (b, s, 0))
    consts = [gmix, w_g, b_g, pool_w, pool_scale, w_po, w_fo, w_o, gx, w_xq, w_xo]
    return pl.pallas_call(
        _post_kernel,
        grid=(B, S // tm),
        in_specs=[row(D), row(POOL_WIDTH), row(FOX_WIDTH),
                  pl.BlockSpec((1, M, 2 * X_WIDTH), lambda b, s: (b, 0, 0))]
                 + [_const_spec(a.shape) for a in consts],
        out_specs=row(D),
        out_shape=jax.ShapeDtypeStruct((B, S, D), F32),
        scratch_shapes=[pltpu.VMEM((tm + MAX_WINDOW, POOL_WIDTH), F32)],
        compiler_params=_params("arbitrary", "arbitrary"),
        name="post",
    )(x, u, o, kv, *consts)


def _ffn_kernel(x_ref, g_ref, wi_ref, wo_ref, gfin_ref, out_ref, *, final_norm):
    d_ff = wo_ref.shape[0]
    x = x_ref[...]
    hf = _rmsnorm(x, g_ref[...]).astype(BF16)
    gu = jnp.dot(hf, wi_ref[...], preferred_element_type=F32)
    gt = gu[:, :d_ff]
    act = (gt * _sigmoid(gt) * gu[:, d_ff:]).astype(BF16)
    x3 = x + jnp.dot(act, wo_ref[...], preferred_element_type=F32)
    out_ref[...] = _rmsnorm(x3, gfin_ref[...]) if final_norm else x3


def _ffn(x2, g, w_i, w_o, g_fin, tm, final_norm):
    T, D = x2.shape
    return pl.pallas_call(
        functools.partial(_ffn_kernel, final_norm=final_norm),
        grid=(T // tm,),
        in_specs=[pl.BlockSpec((tm, D), lambda i: (i, 0)), _const_spec((1, D)),
                  pl.BlockSpec(w_i.shape, lambda i: (0, 0), pipeline_mode=pl.Buffered(1)),
                  pl.BlockSpec(w_o.shape, lambda i: (0, 0), pipeline_mode=pl.Buffered(1)),
                  _const_spec((1, D))],
        out_specs=pl.BlockSpec((tm, D), lambda i: (i, 0)),
        out_shape=jax.ShapeDtypeStruct((T, D), F32),
        compiler_params=_params("arbitrary"),
        name="ffn",
    )(x2, g, w_i, w_o, g_fin)


def kernel(x, mem, norm_mix_g, w_in, b_forget, b_gate, pool_w, pool_scale, w_pool_out, w_fox_out,
           w_out, norm_x_g, norm_mem_g, w_xq, w_xkv, w_xo, norm_ffn_g, w_ffn_in, w_ffn_out,
           norm_final_g):
    B, S, D = x.shape
    depth = w_in.shape[0]
    qkv_end = POOL_WIDTH + 3 * FOX_WIDTH
    f_end = qkv_end + FOX_HEADS
    row2 = lambda a: a.reshape(1, -1)
    for l in range(depth):
        wl = w_in[l].astype(BF16)
        w_f = jnp.pad(wl[:, qkv_end:f_end], ((0, 0), (0, LANES - FOX_HEADS)))
        b_f = jnp.pad(b_forget[l], (0, LANES - FOX_HEADS)).reshape(1, LANES)
        kv = _mem_kv(mem, row2(norm_mem_g[l]), w_xkv[l].astype(BF16))
        u, qa, ka, v = _in_proj(x, row2(norm_mix_g[l]), wl[:, :POOL_WIDTH],
                                wl[:, POOL_WIDTH:qkv_end], w_f, b_f, ROW_TILE)
        o = _fox_attention(qa, ka, v, ATT_TILE)
        x2 = _post(x, u, o, kv, row2(norm_mix_g[l]), wl[:, f_end:], row2(b_gate[l]),
                   pool_w[l].astype(BF16), row2(pool_scale[l]), w_pool_out[l].astype(BF16),
                   w_fox_out[l].astype(BF16), w_out[l].astype(BF16), row2(norm_x_g[l]),
                   w_xq[l].astype(BF16), w_xo[l].astype(BF16), ROW_TILE)
        x = _ffn(x2.reshape(B * S, D), row2(norm_ffn_g[l]), w_ffn_in[l].astype(BF16),
                 w_ffn_out[l].astype(BF16), row2(norm_final_g), ROW_TILE,
                 final_norm=(l == depth - 1)).reshape(B, S, D)
    return x
```

```python
import functools

import jax
import jax.numpy as jnp
from jax import lax
from jax.experimental import pallas as pl
from jax.experimental.pallas import tpu as pltpu

F32 = jnp.float32
BF16 = jnp.bfloat16

EPS = 1e-6
POOL_WINDOWS = (2, 4, 8, 16)
POOL_GC = 128
POOL_WIDTH = 512
MAX_WINDOW = 16
FOX_HEADS = 8
FOX_DH = 64
FOX_WIDTH = 512
X_HEADS = 4
X_DH = 128
X_WIDTH = 512
LANES = 128
VMEM_LIMIT = 56 * 1024 * 1024
NEG_BIG = -1e30
AUG_TERMS = 3

ROW_TILE = 512
ATT_TILE = 256
SCORE_LOOKAHEAD = 4


def _rmsnorm(x, g):
    return x * lax.rsqrt(jnp.mean(x * x, axis=-1, keepdims=True) + EPS) * g


def _log_sigmoid(x):
    return jnp.minimum(x, 0.0) - jnp.log1p(jnp.exp(-jnp.abs(x)))


def _sigmoid(x):
    return 1.0 / (1.0 + jnp.exp(-x))


def _const_spec(shape):
    return pl.BlockSpec(shape, lambda *_: (0,) * len(shape))


def _params(*sem):
    return pltpu.CompilerParams(dimension_semantics=sem, vmem_limit_bytes=VMEM_LIMIT)


def _mem_kv_kernel(mem_ref, g_ref, w_ref, kv_ref):
    mn = _rmsnorm(mem_ref[0], g_ref[...]).astype(BF16)
    kv_ref[0] = jnp.dot(mn, w_ref[...], preferred_element_type=F32).astype(BF16)


def _mem_kv(mem, g, w_xkv):
    B, M, D = mem.shape
    return pl.pallas_call(
        _mem_kv_kernel,
        grid=(B,),
        in_specs=[pl.BlockSpec((1, M, D), lambda b: (b, 0, 0)),
                  _const_spec((1, D)), _const_spec(w_xkv.shape)],
        out_specs=pl.BlockSpec((1, M, 2 * X_WIDTH), lambda b: (b, 0, 0)),
        out_shape=jax.ShapeDtypeStruct((B, M, 2 * X_WIDTH), BF16),
        compiler_params=_params("arbitrary"),
        name="mem_kv",
    )(mem, g, w_xkv)


def _split3(c):
    hi = c.astype(BF16).astype(F32)
    r = c - hi
    lo = r.astype(BF16).astype(F32)
    return hi, lo, r - lo


def _augment(feat, c_col, in_aug, a, query):
    pieces = _split3(c_col)
    ones_lo = AUG_TERMS if query else 0
    aug = jnp.where((a >= ones_lo) & (a < ones_lo + AUG_TERMS), 1.0, 0.0)
    for i, piece in enumerate(pieces):
        aug = jnp.where(a == (i if query else AUG_TERMS + i), piece if query else -piece, aug)
    return jnp.where(in_aug, aug, feat).astype(BF16)


def _in_proj_kernel(x_ref, g_ref, wu_ref, wqkv_ref, wf_ref, bf_ref,
                    u_ref, qa_ref, ka_ref, vt_ref, csum_ref):
    tm = x_ref.shape[1]

    @pl.when(pl.program_id(1) == 0)
    def _():
        csum_ref[...] = jnp.zeros_like(csum_ref)

    h = _rmsnorm(x_ref[0], g_ref[...]).astype(BF16)
    u_ref[0] = jnp.dot(h, wu_ref[...], preferred_element_type=F32)
    qkv = jnp.dot(h, wqkv_ref[...], preferred_element_type=F32)
    ta = vt_ref.shape[3]
    for sub in range(tm // ta):
        vt_ref[0, sub] = qkv[sub * ta:(sub + 1) * ta, 2 * FOX_WIDTH:].T.astype(BF16)

    logf = _log_sigmoid(jnp.dot(h, wf_ref[...], preferred_element_type=F32) + bf_ref[...])
    row = lax.broadcasted_iota(jnp.int32, (tm, tm), 0)
    col = lax.broadcasted_iota(jnp.int32, (tm, tm), 1)
    tri = jnp.where(row >= col, 1.0, 0.0).astype(BF16)
    parts = jnp.dot(tri, jnp.concatenate(_split3(logf), axis=1).astype(BF16),
                    preferred_element_type=F32)
    c = parts[:, :LANES] + parts[:, LANES:2 * LANES] + parts[:, 2 * LANES:] + csum_ref[...]
    csum_ref[...] = c[tm - 1:tm, :]

    lane = lax.broadcasted_iota(jnp.int32, (1, LANES), 1)
    for pair in range(FOX_HEADS // 2):
        qp = qkv[:, pair * LANES:(pair + 1) * LANES] * (FOX_DH ** -0.5)
        kp = qkv[:, FOX_WIDTH + pair * LANES:FOX_WIDTH + (pair + 1) * LANES]
        for hh in range(2):
            head = 2 * pair + hh
            in_aug = (lane >= FOX_DH) if hh == 0 else (lane < FOX_DH)
            a = (lane - FOX_DH) if hh == 0 else lane
            c_col = c[:, head:head + 1]
            hs = slice(head * LANES, (head + 1) * LANES)
            qa_ref[0, :, hs] = _augment(qp, c_col, in_aug, a, query=True)
            ka_ref[0, :, hs] = _augment(kp, c_col, in_aug, a, query=False)


def _in_proj(x, g, w_u, w_qkv, w_f, b_f, tm, ta):
    B, S, D = x.shape
    row = lambda n: pl.BlockSpec((1, tm, n), lambda b, s: (b, s, 0))
    return pl.pallas_call(
        _in_proj_kernel,
        grid=(B, S // tm),
        in_specs=[row(D), _const_spec((1, D)), _const_spec(w_u.shape), _const_spec(w_qkv.shape),
                  _const_spec(w_f.shape), _const_spec((1, LANES))],
        out_specs=[row(POOL_WIDTH), row(FOX_HEADS * LANES), row(FOX_HEADS * LANES),
                   pl.BlockSpec((1, tm // ta, FOX_WIDTH, ta), lambda b, s: (b, s, 0, 0))],
        out_shape=[jax.ShapeDtypeStruct((B, S, POOL_WIDTH), F32),
                   jax.ShapeDtypeStruct((B, S, FOX_HEADS * LANES), BF16),
                   jax.ShapeDtypeStruct((B, S, FOX_HEADS * LANES), BF16),
                   jax.ShapeDtypeStruct((B, S // ta, FOX_WIDTH, ta), BF16)],
        scratch_shapes=[pltpu.VMEM((1, LANES), F32)],
        compiler_params=_params("arbitrary", "arbitrary"),
        name="in_proj",
    )(x, g, w_u, w_qkv, w_f, b_f)


def _fox_kernel(q_ref, k_ref, vt_ref, o_ref):
    t = q_ref.shape[1]
    qi = pl.program_id(1)
    causal_t = (lax.broadcasted_iota(jnp.int32, (t, t), 0) <=
                lax.broadcasted_iota(jnp.int32, (t, t), 1))
    nt_dims = (((1,), (1,)), ((), ()))

    def block(j, carry, masked):
        ms, ls, accs = carry
        rows = pl.ds(pl.multiple_of(j * t, t), t)

        def scores(hd):
            hs = slice(hd * LANES, (hd + 1) * LANES)
            return lax.dot_general(k_ref[0, rows, hs], q_ref[0, :, hs], nt_dims,
                                   preferred_element_type=F32)

        new_m, new_l, new_acc = [], [], []
        pending = [scores(hd) for hd in range(SCORE_LOOKAHEAD)]
        for hd in range(FOX_HEADS):
            st = pending.pop(0)
            if hd + SCORE_LOOKAHEAD < FOX_HEADS:
                pending.append(scores(hd + SCORE_LOOKAHEAD))
            if masked:
                st = jnp.where(causal_t, st, NEG_BIG)
            m_new = jnp.maximum(ms[hd], jnp.max(st, axis=0, keepdims=True))
            alpha = jnp.exp(ms[hd] - m_new)
            pt = jnp.exp(st - m_new)
            new_m.append(m_new)
            new_l.append(alpha * ls[hd] + jnp.sum(pt, axis=0, keepdims=True))
            pv = jnp.dot(vt_ref[0, j, hd * FOX_DH:(hd + 1) * FOX_DH, :], pt.astype(BF16),
                         preferred_element_type=F32)
            new_acc.append(alpha * accs[hd] + pv)
        return tuple(new_m), tuple(new_l), tuple(new_acc)

    init = (tuple(jnp.full((1, t), NEG_BIG, F32) for _ in range(FOX_HEADS)),
            tuple(jnp.zeros((1, t), F32) for _ in range(FOX_HEADS)),
            tuple(jnp.zeros((FOX_DH, t), F32) for _ in range(FOX_HEADS)))
    carry = lax.fori_loop(0, qi, functools.partial(block, masked=False), init)
    _, ls, accs = block(qi, carry, masked=True)
    o_t = jnp.concatenate([accs[hd] / ls[hd] for hd in range(FOX_HEADS)], axis=0)
    o_ref[0] = o_t.T.astype(BF16)


def _fox_attention(qa, ka, vt, t):
    B, S, _ = qa.shape
    return pl.pallas_call(
        _fox_kernel,
        grid=(B, S // t),
        in_specs=[pl.BlockSpec((1, t, FOX_HEADS * LANES), lambda b, i: (b, i, 0)),
                  pl.BlockSpec((1, S, FOX_HEADS * LANES), lambda b, i: (b, 0, 0)),
                  pl.BlockSpec((1, S // t, FOX_WIDTH, t), lambda b, i: (b, 0, 0, 0))],
        out_specs=pl.BlockSpec((1, t, FOX_WIDTH), lambda b, i: (b, i, 0)),
        out_shape=jax.ShapeDtypeStruct((B, S, FOX_WIDTH), BF16),
        compiler_params=_params("arbitrary", "arbitrary"),
        name="fox_attn",
    )(qa, ka, vt)


def _post_kernel(x_ref, u_ref, o_ref, kv_ref, gmix_ref, wg_ref, bg_ref, pw_ref, ps_ref, wpo_ref,
                 wfo_ref, wo_ref, gx_ref, wxq_ref, wxo_ref, x2_ref, ubuf_ref):
    tm = x_ref.shape[1]
    D = x_ref.shape[2]
    si = pl.program_id(1)
    x = x_ref[0]

    @pl.when(si == 0)
    def _():
        ubuf_ref[0:MAX_WINDOW, :] = jnp.zeros((MAX_WINDOW, POOL_WIDTH), F32)

    @pl.when(si > 0)
    def _():
        ubuf_ref[0:MAX_WINDOW, :] = ubuf_ref[tm:tm + MAX_WINDOW, :]

    u = u_ref[0]
    ubuf_ref[MAX_WINDOW:, :] = u
    pos = si * tm + lax.broadcasted_iota(jnp.int32, (tm, 1), 0)
    ys = []
    for g, w in enumerate(POOL_WINDOWS):
        sl = slice(g * POOL_GC, (g + 1) * POOL_GC)
        ug = u[:, sl]
        ws = ug
        for j in range(1, w):
            ws = ws + ubuf_ref[MAX_WINDOW - j:MAX_WINDOW - j + tm, sl]
        cnt = jnp.minimum(pos + 1, w).astype(F32)
        d = (ws / cnt - ug).astype(BF16)
        ys.append(jnp.dot(d, pw_ref[g], preferred_element_type=F32))
    y = (jnp.concatenate(ys, axis=1) * ps_ref[...]).astype(BF16)
    y_pool = jnp.dot(y, wpo_ref[...], preferred_element_type=F32)

    h = _rmsnorm(x, gmix_ref[...]).astype(BF16)
    gates = _sigmoid(jnp.dot(h, wg_ref[...], preferred_element_type=F32) + bg_ref[...])
    y_fox = jnp.dot(o_ref[0], wfo_ref[...], preferred_element_type=F32)
    mix = (gates[:, :D] * y_pool + gates[:, D:] * y_fox).astype(BF16)
    x1 = x + jnp.dot(mix, wo_ref[...], preferred_element_type=F32)

    hx = _rmsnorm(x1, gx_ref[...]).astype(BF16)
    qx = jnp.dot(hx, wxq_ref[...], preferred_element_type=F32).astype(BF16)
    kv = kv_ref[0]
    heads = []
    for hd in range(X_HEADS):
        sl = slice(hd * X_DH, (hd + 1) * X_DH)
        s = lax.dot_general(qx[:, sl], kv[:, sl], (((1,), (1,)), ((), ())),
                            preferred_element_type=F32) * (X_DH ** -0.5)
        p = jnp.exp(s - jnp.max(s, axis=1, keepdims=True))
        l = jnp.sum(p, axis=1, keepdims=True)
        vh = kv[:, X_WIDTH + hd * X_DH:X_WIDTH + (hd + 1) * X_DH]
        heads.append(jnp.dot(p.astype(BF16), vh, preferred_element_type=F32) / l)
    ox = jnp.concatenate(heads, axis=1).astype(BF16)
    x2_ref[0] = x1 + jnp.dot(ox, wxo_ref[...], preferred_element_type=F32)


def _post(x, u, o, kv, gmix, w_g, b_g, pool_w, pool_scale, w_po, w_fo, w_o, gx, w_xq, w_xo, tm):
    B, S, D = x.shape
    M = kv.shape[1]
    row = lambda n: pl.BlockSpec((1, tm, n), lambda b, s: (b, s, 0))
    consts = [gmix, w_g, b_g, pool_w, pool_scale, w_po, w_fo, w_o, gx, w_xq, w_xo]
    return pl.pallas_call(
        _post_kernel,
        grid=(B, S // tm),
        in_specs=[row(D), row(POOL_WIDTH), row(FOX_WIDTH),
                  pl.BlockSpec((1, M, 2 * X_WIDTH), lambda b, s: (b, 0, 0))]
                 + [_const_spec(a.shape) for a in consts],
        out_specs=row(D),
        out_shape=jax.ShapeDtypeStruct((B, S, D), F32),
        scratch_shapes=[pltpu.VMEM((tm + MAX_WINDOW, POOL_WIDTH), F32)],
        compiler_params=_params("arbitrary", "arbitrary"),
        name="post",
    )(x, u, o, kv, *consts)


def _ffn_kernel(x_ref, g_ref, wi_ref, wo_ref, gfin_ref, out_ref, *, final_norm):
    d_ff = wo_ref.shape[0]
    x = x_ref[...]
    hf = _rmsnorm(x, g_ref[...]).astype(BF16)
    gu = jnp.dot(hf, wi_ref[...], preferred_element_type=F32)
    gt = gu[:, :d_ff]
    act = (gt * _sigmoid(gt) * gu[:, d_ff:]).astype(BF16)
    x3 = x + jnp.dot(act, wo_ref[...], preferred_element_type=F32)
    out_ref[...] = _rmsnorm(x3, gfin_ref[...]) if final_norm else x3


def _ffn(x2, g, w_i, w_o, g_fin, tm, final_norm):
    T, D = x2.shape
    return pl.pallas_call(
        functools.partial(_ffn_kernel, final_norm=final_norm),
        grid=(T // tm,),
        in_specs=[pl.BlockSpec((tm, D), lambda i: (i, 0)), _const_spec((1, D)),
                  pl.BlockSpec(w_i.shape, lambda i: (0, 0), pipeline_mode=pl.Buffered(1)),
                  pl.BlockSpec(w_o.shape, lambda i: (0, 0), pipeline_mode=pl.Buffered(1)),
                  _const_spec((1, D))],
        out_specs=pl.BlockSpec((tm, D), lambda i: (i, 0)),
        out_shape=jax.ShapeDtypeStruct((T, D), F32),
        compiler_params=_params("arbitrary"),
        name="ffn",
    )(x2, g, w_i, w_o, g_fin)


def kernel(x, mem, norm_mix_g, w_in, b_forget, b_gate, pool_w, pool_scale, w_pool_out, w_fox_out,
           w_out, norm_x_g, norm_mem_g, w_xq, w_xkv, w_xo, norm_ffn_g, w_ffn_in, w_ffn_out,
           norm_final_g):
    B, S, D = x.shape
    depth = w_in.shape[0]
    qkv_end = POOL_WIDTH + 3 * FOX_WIDTH
    f_end = qkv_end + FOX_HEADS
    row2 = lambda a: a.reshape(1, -1)
    for l in range(depth):
        wl = w_in[l].astype(BF16)
        w_f = jnp.pad(wl[:, qkv_end:f_end], ((0, 0), (0, LANES - FOX_HEADS)))
        b_f = jnp.pad(b_forget[l], (0, LANES - FOX_HEADS)).reshape(1, LANES)
        kv = _mem_kv(mem, row2(norm_mem_g[l]), w_xkv[l].astype(BF16))
        u, qa, ka, vt = _in_proj(x, row2(norm_mix_g[l]), wl[:, :POOL_WIDTH],
                                 wl[:, POOL_WIDTH:qkv_end], w_f, b_f, ROW_TILE, ATT_TILE)
        o = _fox_attention(qa, ka, vt, ATT_TILE)
        x2 = _post(x, u, o, kv, row2(norm_mix_g[l]), wl[:, f_end:], row2(b_gate[l]),
                   pool_w[l].astype(BF16), row2(pool_scale[l]), w_pool_out[l].astype(BF16),
                   w_fox_out[l].astype(BF16), w_out[l].astype(BF16), row2(norm_x_g[l]),
                   w_xq[l].astype(BF16), w_xo[l].astype(BF16), ROW_TILE)
        x = _ffn(x2.reshape(B * S, D), row2(norm_ffn_g[l]), w_ffn_in[l].astype(BF16),
                 w_ffn_out[l].astype(BF16), row2(norm_final_g), ROW_TILE,
                 final_norm=(l == depth - 1)).reshape(B, S, D)
    return x
```

```python
import functools

import jax
import jax.numpy as jnp
from jax import lax
from jax.experimental import pallas as pl
from jax.experimental.pallas import tpu as pltpu

F32 = jnp.float32
BF16 = jnp.bfloat16

EPS = 1e-6
POOL_WINDOWS = (2, 4, 8, 16)
POOL_GC = 128
POOL_WIDTH = 512
MAX_WINDOW = 16
FOX_HEADS = 8
FOX_DH = 64
FOX_WIDTH = 512
X_HEADS = 4
X_DH = 128
X_WIDTH = 512
LANES = 128
MXU_COLS = 256
VMEM_LIMIT = 56 * 1024 * 1024
NEG_BIG = -1e30
AUG_TERMS = 3

ROW_TILE = 512
ATT_TILE = 256
SCORE_LOOKAHEAD = 4


def _rmsnorm(x, g):
    return x * lax.rsqrt(jnp.mean(x * x, axis=-1, keepdims=True) + EPS) * g


def _log_sigmoid(x):
    return jnp.minimum(x, 0.0) - jnp.log1p(jnp.exp(-jnp.abs(x)))


def _sigmoid(x):
    return 1.0 / (1.0 + jnp.exp(-x))


def _const_spec(shape):
    return pl.BlockSpec(shape, lambda *_: (0,) * len(shape))


def _params(*sem):
    return pltpu.CompilerParams(dimension_semantics=sem, vmem_limit_bytes=VMEM_LIMIT)


def _mem_kv_kernel(mem_ref, g_ref, w_ref, kv_ref):
    mn = _rmsnorm(mem_ref[0], g_ref[...]).astype(BF16)
    kv_ref[0] = jnp.dot(mn, w_ref[...], preferred_element_type=F32).astype(BF16)


def _mem_kv(mem, g, w_xkv):
    B, M, D = mem.shape
    return pl.pallas_call(
        _mem_kv_kernel,
        grid=(B,),
        in_specs=[pl.BlockSpec((1, M, D), lambda b: (b, 0, 0)),
                  _const_spec((1, D)), _const_spec(w_xkv.shape)],
        out_specs=pl.BlockSpec((1, M, 2 * X_WIDTH), lambda b: (b, 0, 0)),
        out_shape=jax.ShapeDtypeStruct((B, M, 2 * X_WIDTH), BF16),
        compiler_params=_params("arbitrary"),
        name="mem_kv",
    )(mem, g, w_xkv)


def _split3(c):
    hi = c.astype(BF16).astype(F32)
    r = c - hi
    lo = r.astype(BF16).astype(F32)
    return hi, lo, r - lo


def _in_proj_kernel(x_ref, g_ref, wu_ref, wqkv_ref, wf_ref, bf_ref,
                    u_ref, qa_ref, ka_ref, vt_ref, csum_ref):
    tm = x_ref.shape[1]

    @pl.when(pl.program_id(1) == 0)
    def _():
        csum_ref[...] = jnp.zeros_like(csum_ref)

    h = _rmsnorm(x_ref[0], g_ref[...]).astype(BF16)

    logf = _log_sigmoid(jnp.dot(h, wf_ref[...], preferred_element_type=F32) + bf_ref[...])

    ta = vt_ref.shape[3]
    v = jnp.dot(h, wqkv_ref[:, 2 * FOX_WIDTH:], preferred_element_type=F32)
    for sub in range(tm // ta):
        vt_ref[0, sub] = v[sub * ta:(sub + 1) * ta, :].T.astype(BF16)

    row = lax.broadcasted_iota(jnp.int32, (tm, tm), 0)
    col = lax.broadcasted_iota(jnp.int32, (tm, tm), 1)
    tri = jnp.where(row >= col, 1.0, 0.0).astype(BF16)
    parts = jnp.dot(tri, jnp.concatenate(_split3(logf), axis=1).astype(BF16),
                    preferred_element_type=F32)
    c = parts[:, :LANES] + parts[:, LANES:2 * LANES] + parts[:, 2 * LANES:] + csum_ref[...]
    csum_ref[...] = c[tm - 1:tm, :]

    lane = lax.broadcasted_iota(jnp.int32, (1, LANES), 1)
    piece = lane % AUG_TERMS
    c_hi, c_lo, c_lo2 = _split3(c)
    c3 = jnp.where(piece == 0, c_hi, jnp.where(piece == 1, c_lo, c_lo2))
    c3_neg = -c3

    heads_per_chunk = MXU_COLS // FOX_DH
    for chunk in range(FOX_WIDTH // MXU_COLS):
        cols = slice(chunk * MXU_COLS, (chunk + 1) * MXU_COLS)
        qc = jnp.dot(h, wqkv_ref[:, cols], preferred_element_type=F32) * (FOX_DH ** -0.5)
        kc = jnp.dot(h, wqkv_ref[:, FOX_WIDTH + chunk * MXU_COLS:FOX_WIDTH + (chunk + 1) * MXU_COLS],
                     preferred_element_type=F32)
        for sub_head in range(heads_per_chunk):
            head = chunk * heads_per_chunk + sub_head
            pr, hh = divmod(sub_head, 2)
            base = FOX_DH if hh == 0 else 0
            a = lane - base
            in_aug = (a >= 0) & (a < FOX_DH)
            rq = pltpu.roll(c3, (base - AUG_TERMS * head) % LANES, 1)
            rk = pltpu.roll(c3_neg, (base + AUG_TERMS - AUG_TERMS * head) % LANES, 1)
            q_fill = jnp.where((a >= AUG_TERMS) & (a < 2 * AUG_TERMS), 1.0, 0.0)
            k_fill = jnp.where((a >= 0) & (a < AUG_TERMS), 1.0, 0.0)
            qp = qc[:, pr * LANES:(pr + 1) * LANES]
            kp = kc[:, pr * LANES:(pr + 1) * LANES]
            hs = slice(head * LANES, (head + 1) * LANES)
            qa_ref[0, :, hs] = jnp.where((a >= 0) & (a < AUG_TERMS), rq,
                                         jnp.where(in_aug, q_fill, qp)).astype(BF16)
            ka_ref[0, :, hs] = jnp.where((a >= AUG_TERMS) & (a < 2 * AUG_TERMS), rk,
                                         jnp.where(in_aug, k_fill, kp)).astype(BF16)

    u_ref[0] = jnp.dot(h, wu_ref[...], preferred_element_type=F32)


def _in_proj(x, g, w_u, w_qkv, w_f, b_f, tm, ta):
    B, S, D = x.shape
    row = lambda n: pl.BlockSpec((1, tm, n), lambda b, s: (b, s, 0))
    return pl.pallas_call(
        _in_proj_kernel,
        grid=(B, S // tm),
        in_specs=[row(D), _const_spec((1, D)), _const_spec(w_u.shape), _const_spec(w_qkv.shape),
                  _const_spec(w_f.shape), _const_spec((1, LANES))],
        out_specs=[row(POOL_WIDTH), row(FOX_HEADS * LANES), row(FOX_HEADS * LANES),
                   pl.BlockSpec((1, tm // ta, FOX_WIDTH, ta), lambda b, s: (b, s, 0, 0))],
        out_shape=[jax.ShapeDtypeStruct((B, S, POOL_WIDTH), F32),
                   jax.ShapeDtypeStruct((B, S, FOX_HEADS * LANES), BF16),
                   jax.ShapeDtypeStruct((B, S, FOX_HEADS * LANES), BF16),
                   jax.ShapeDtypeStruct((B, S // ta, FOX_WIDTH, ta), BF16)],
        scratch_shapes=[pltpu.VMEM((1, LANES), F32)],
        compiler_params=_params("arbitrary", "arbitrary"),
        name="in_proj",
    )(x, g, w_u, w_qkv, w_f, b_f)


def _fox_kernel(q_ref, k_ref, vt_ref, o_ref):
    t = q_ref.shape[1]
    qi = pl.program_id(1)
    causal_t = (lax.broadcasted_iota(jnp.int32, (t, t), 0) <=
                lax.broadcasted_iota(jnp.int32, (t, t), 1))
    nt_dims = (((1,), (1,)), ((), ()))

    def block(j, carry, masked):
        ms, ls, accs = carry
        rows = pl.ds(pl.multiple_of(j * t, t), t)

        def scores(hd):
            hs = slice(hd * LANES, (hd + 1) * LANES)
            return lax.dot_general(k_ref[0, rows, hs], q_ref[0, :, hs], nt_dims,
                                   preferred_element_type=F32)

        new_m, new_l, new_acc = [], [], []
        pending = [scores(hd) for hd in range(SCORE_LOOKAHEAD)]
        for hd in range(FOX_HEADS):
            st = pending.pop(0)
            if hd + SCORE_LOOKAHEAD < FOX_HEADS:
                pending.append(scores(hd + SCORE_LOOKAHEAD))
            if masked:
                st = jnp.where(causal_t, st, NEG_BIG)
            m_new = jnp.maximum(ms[hd], jnp.max(st, axis=0, keepdims=True))
            alpha = jnp.exp(ms[hd] - m_new)
            pt = jnp.exp(st - m_new)
            new_m.append(m_new)
            new_l.append(alpha * ls[hd] + jnp.sum(pt, axis=0, keepdims=True))
            pv = jnp.dot(vt_ref[0, j, hd * FOX_DH:(hd + 1) * FOX_DH, :], pt.astype(BF16),
                         preferred_element_type=F32)
            new_acc.append(alpha * accs[hd] + pv)
        return tuple(new_m), tuple(new_l), tuple(new_acc)

    init = (tuple(jnp.full((1, t), NEG_BIG, F32) for _ in range(FOX_HEADS)),
            tuple(jnp.zeros((1, t), F32) for _ in range(FOX_HEADS)),
            tuple(jnp.zeros((FOX_DH, t), F32) for _ in range(FOX_HEADS)))
    carry = lax.fori_loop(0, qi, functools.partial(block, masked=False), init)
    _, ls, accs = block(qi, carry, masked=True)
    o_t = jnp.concatenate([accs[hd] / ls[hd] for hd in range(FOX_HEADS)], axis=0)
    o_ref[0] = o_t.T.astype(BF16)


def _fox_attention(qa, ka, vt, t):
    B, S, _ = qa.shape
    return pl.pallas_call(
        _fox_kernel,
        grid=(B, S // t),
        in_specs=[pl.BlockSpec((1, t, FOX_HEADS * LANES), lambda b, i: (b, i, 0)),
                  pl.BlockSpec((1, S, FOX_HEADS * LANES), lambda b, i: (b, 0, 0)),
                  pl.BlockSpec((1, S // t, FOX_WIDTH, t), lambda b, i: (b, 0, 0, 0))],
        out_specs=pl.BlockSpec((1, t, FOX_WIDTH), lambda b, i: (b, i, 0)),
        out_shape=jax.ShapeDtypeStruct((B, S, FOX_WIDTH), BF16),
        compiler_params=_params("arbitrary", "arbitrary"),
        name="fox_attn",
    )(qa, ka, vt)


def _post_kernel(x_ref, u_ref, o_ref, kv_ref, gmix_ref, wg_ref, bg_ref, pw_ref, ps_ref, wpo_ref,
                 wfo_ref, wo_ref, gx_ref, wxq_ref, wxo_ref, x2_ref, ubuf_ref):
    tm = x_ref.shape[1]
    D = x_ref.shape[2]
    si = pl.program_id(1)
    x = x_ref[0]

    @pl.when(si == 0)
    def _():
        ubuf_ref[0:MAX_WINDOW, :] = jnp.zeros((MAX_WINDOW, POOL_WIDTH), F32)

    @pl.when(si > 0)
    def _():
        ubuf_ref[0:MAX_WINDOW, :] = ubuf_ref[tm:tm + MAX_WINDOW, :]

    u = u_ref[0]
    ubuf_ref[MAX_WINDOW:, :] = u
    pos = si * tm + lax.broadcasted_iota(jnp.int32, (tm, 1), 0)
    ys = []
    for g, w in enumerate(POOL_WINDOWS):
        sl = slice(g * POOL_GC, (g + 1) * POOL_GC)
        ug = u[:, sl]
        ws = ug
        for j in range(1, w):
            ws = ws + ubuf_ref[MAX_WINDOW - j:MAX_WINDOW - j + tm, sl]
        cnt = jnp.minimum(pos + 1, w).astype(F32)
        d = (ws / cnt - ug).astype(BF16)
        ys.append(jnp.dot(d, pw_ref[g], preferred_element_type=F32))
    y = (jnp.concatenate(ys, axis=1) * ps_ref[...]).astype(BF16)
    y_pool = jnp.dot(y, wpo_ref[...], preferred_element_type=F32)

    h = _rmsnorm(x, gmix_ref[...]).astype(BF16)
    gates = _sigmoid(jnp.dot(h, wg_ref[...], preferred_element_type=F32) + bg_ref[...])
    y_fox = jnp.dot(o_ref[0], wfo_ref[...], preferred_element_type=F32)
    mix = (gates[:, :D] * y_pool + gates[:, D:] * y_fox).astype(BF16)
    x1 = x + jnp.dot(mix, wo_ref[...], preferred_element_type=F32)

    hx = _rmsnorm(x1, gx_ref[...]).astype(BF16)
    qx = jnp.dot(hx, wxq_ref[...], preferred_element_type=F32).astype(BF16)
    kv = kv_ref[0]
    heads = []
    for hd in range(X_HEADS):
        sl = slice(hd * X_DH, (hd + 1) * X_DH)
        s = lax.dot_general(qx[:, sl], kv[:, sl], (((1,), (1,)), ((), ())),
                            preferred_element_type=F32) * (X_DH ** -0.5)
        p = jnp.exp(s - jnp.max(s, axis=1, keepdims=True))
        l = jnp.sum(p, axis=1, keepdims=True)
        vh = kv[:, X_WIDTH + hd * X_DH:X_WIDTH + (hd + 1) * X_DH]
        heads.append(jnp.dot(p.astype(BF16), vh, preferred_element_type=F32) / l)
    ox = jnp.concatenate(heads, axis=1).astype(BF16)
    x2_ref[0] = x1 + jnp.dot(ox, wxo_ref[...], preferred_element_type=F32)


def _post(x, u, o, kv, gmix, w_g, b_g, pool_w, pool_scale, w_po, w_fo, w_o, gx, w_xq, w_xo, tm):
    B, S, D = x.shape
    M = kv.shape[1]
    row = lambda n: pl.BlockSpec((1, tm, n), lambda b, s: (b, s, 0))
    consts = [gmix, w_g, b_g, pool_w, pool_scale, w_po, w_fo, w_o, gx, w_xq, w_xo]
    return pl.pallas_call(
        _post_kernel,
        grid=(B, S // tm),
        in_specs=[row(D), row(POOL_WIDTH), row(FOX_WIDTH),
                  pl.BlockSpec((1, M, 2 * X_WIDTH), lambda b, s: (b, 0, 0))]
                 + [_const_spec(a.shape) for a in consts],
        out_specs=row(D),
        out_shape=jax.ShapeDtypeStruct((B, S, D), F32),
        scratch_shapes=[pltpu.VMEM((tm + MAX_WINDOW, POOL_WIDTH), F32)],
        compiler_params=_params("arbitrary", "arbitrary"),
        name="post",
    )(x, u, o, kv, *consts)


def _ffn_kernel(x_ref, g_ref, wi_ref, wo_ref, gfin_ref, out_ref, *, final_norm):
    d_ff = wo_ref.shape[0]
    x = x_ref[...]
    hf = _rmsnorm(x, g_ref[...]).astype(BF16)
    gu = jnp.dot(hf, wi_ref[...], preferred_element_type=F32)
    gt = gu[:, :d_ff]
    act = (gt * _sigmoid(gt) * gu[:, d_ff:]).astype(BF16)
    x3 = x + jnp.dot(act, wo_ref[...], preferred_element_type=F32)
    out_ref[...] = _rmsnorm(x3, gfin_ref[...]) if final_norm else x3


def _ffn(x2, g, w_i, w_o, g_fin, tm, final_norm):
    T, D = x2.shape
    return pl.pallas_call(
        functools.partial(_ffn_kernel, final_norm=final_norm),
        grid=(T // tm,),
        in_specs=[pl.BlockSpec((tm, D), lambda i: (i, 0)), _const_spec((1, D)),
                  pl.BlockSpec(w_i.shape, lambda i: (0, 0), pipeline_mode=pl.Buffered(1)),
                  pl.BlockSpec(w_o.shape, lambda i: (0, 0), pipeline_mode=pl.Buffered(1)),
                  _const_spec((1, D))],
        out_specs=pl.BlockSpec((tm, D), lambda i: (i, 0)),
        out_shape=jax.ShapeDtypeStruct((T, D), F32),
        compiler_params=_params("arbitrary"),
        name="ffn",
    )(x2, g, w_i, w_o, g_fin)


def kernel(x, mem, norm_mix_g, w_in, b_forget, b_gate, pool_w, pool_scale, w_pool_out, w_fox_out,
           w_out, norm_x_g, norm_mem_g, w_xq, w_xkv, w_xo, norm_ffn_g, w_ffn_in, w_ffn_out,
           norm_final_g):
    B, S, D = x.shape
    depth = w_in.shape[0]
    qkv_end = POOL_WIDTH + 3 * FOX_WIDTH
    f_end = qkv_end + FOX_HEADS
    row2 = lambda a: a.reshape(1, -1)
    for l in range(depth):
        wl = w_in[l].astype(BF16)
        rep = AUG_TERMS * FOX_HEADS
        w_f = jnp.pad(jnp.repeat(wl[:, qkv_end:f_end], AUG_TERMS, axis=1), ((0, 0), (0, LANES - rep)))
        b_f = jnp.pad(jnp.repeat(b_forget[l], AUG_TERMS), (0, LANES - rep)).reshape(1, LANES)
        kv = _mem_kv(mem, row2(norm_mem_g[l]), w_xkv[l].astype(BF16))
        u, qa, ka, vt = _in_proj(x, row2(norm_mix_g[l]), wl[:, :POOL_WIDTH],
                                 wl[:, POOL_WIDTH:qkv_end], w_f, b_f, ROW_TILE, ATT_TILE)
        o = _fox_attention(qa, ka, vt, ATT_TILE)
        x2 = _post(x, u, o, kv, row2(norm_mix_g[l]), wl[:, f_end:], row2(b_gate[l]),
                   pool_w[l].astype(BF16), row2(pool_scale[l]), w_pool_out[l].astype(BF16),
                   w_fox_out[l].astype(BF16), w_out[l].astype(BF16), row2(norm_x_g[l]),
                   w_xq[l].astype(BF16), w_xo[l].astype(BF16), ROW_TILE)
        x = _ffn(x2.reshape(B * S, D), row2(norm_ffn_g[l]), w_ffn_in[l].astype(BF16),
                 w_ffn_out[l].astype(BF16), row2(norm_final_g), ROW_TILE,
                 final_norm=(l == depth - 1)).reshape(B, S, D)
    return x
```

```python
import functools

import jax
import jax.numpy as jnp
from jax import lax
from jax.experimental import pallas as pl
from jax.experimental.pallas import tpu as pltpu

F32 = jnp.float32
BF16 = jnp.bfloat16

EPS = 1e-6
POOL_WINDOWS = (2, 4, 8, 16)
POOL_GC = 128
POOL_WIDTH = 512
MAX_WINDOW = 16
FOX_HEADS = 8
FOX_DH = 64
FOX_WIDTH = 512
X_HEADS = 4
X_DH = 128
X_WIDTH = 512
LANES = 128
MXU_COLS = 256
VMEM_LIMIT = 56 * 1024 * 1024
NEG_BIG = -1e30
AUG_TERMS = 3
LOG2_E = 1.4426950408889634
V_ROWS = FOX_DH + 16

ROW_TILE = 512
ATT_TILE = 256
SCORE_LOOKAHEAD = 4


def _rmsnorm(x, g):
    return x * lax.rsqrt(jnp.mean(x * x, axis=-1, keepdims=True) + EPS) * g


def _log_sigmoid(x):
    return jnp.minimum(x, 0.0) - jnp.log1p(jnp.exp(-jnp.abs(x)))


def _sigmoid(x):
    return 1.0 / (1.0 + jnp.exp(-x))


def _const_spec(shape):
    return pl.BlockSpec(shape, lambda *_: (0,) * len(shape))


def _params(*sem):
    return pltpu.CompilerParams(dimension_semantics=sem, vmem_limit_bytes=VMEM_LIMIT)


def _mem_kv_kernel(mem_ref, g_ref, w_ref, kv_ref):
    mn = _rmsnorm(mem_ref[0], g_ref[...]).astype(BF16)
    kv_ref[0] = jnp.dot(mn, w_ref[...], preferred_element_type=F32).astype(BF16)


def _mem_kv(mem, g, w_xkv):
    B, M, D = mem.shape
    return pl.pallas_call(
        _mem_kv_kernel,
        grid=(B,),
        in_specs=[pl.BlockSpec((1, M, D), lambda b: (b, 0, 0)),
                  _const_spec((1, D)), _const_spec(w_xkv.shape)],
        out_specs=pl.BlockSpec((1, M, 2 * X_WIDTH), lambda b: (b, 0, 0)),
        out_shape=jax.ShapeDtypeStruct((B, M, 2 * X_WIDTH), BF16),
        compiler_params=_params("arbitrary"),
        name="mem_kv",
    )(mem, g, w_xkv)


def _split3(c):
    hi = c.astype(BF16).astype(F32)
    r = c - hi
    lo = r.astype(BF16).astype(F32)
    return hi, lo, r - lo


def _in_proj_kernel(x_ref, g_ref, wu_ref, wqkv_ref, wf_ref, bf_ref,
                    u_ref, qa_ref, ka_ref, vt_ref, csum_ref):
    tm = x_ref.shape[1]

    @pl.when(pl.program_id(1) == 0)
    def _():
        csum_ref[...] = jnp.zeros_like(csum_ref)

    h = _rmsnorm(x_ref[0], g_ref[...]).astype(BF16)

    logf = _log_sigmoid(jnp.dot(h, wf_ref[...], preferred_element_type=F32) + bf_ref[...])

    ta = vt_ref.shape[3]
    v = jnp.dot(h, wqkv_ref[:, 2 * FOX_WIDTH:], preferred_element_type=F32)
    ones_rows = jnp.where(lax.broadcasted_iota(jnp.int32, (V_ROWS - FOX_DH, ta), 0) == 0,
                          1.0, 0.0).astype(BF16)
    for sub in range(tm // ta):
        v_t = v[sub * ta:(sub + 1) * ta, :].T.astype(BF16)
        for hd in range(FOX_HEADS):
            vt_ref[0, sub, hd * V_ROWS:hd * V_ROWS + FOX_DH, :] = v_t[hd * FOX_DH:(hd + 1) * FOX_DH, :]
            vt_ref[0, sub, hd * V_ROWS + FOX_DH:(hd + 1) * V_ROWS, :] = ones_rows

    row = lax.broadcasted_iota(jnp.int32, (tm, tm), 0)
    col = lax.broadcasted_iota(jnp.int32, (tm, tm), 1)
    tri = jnp.where(row >= col, 1.0, 0.0).astype(BF16)
    parts = jnp.dot(tri, jnp.concatenate(_split3(logf), axis=1).astype(BF16),
                    preferred_element_type=F32)
    c = parts[:, :LANES] + parts[:, LANES:2 * LANES] + parts[:, 2 * LANES:] + csum_ref[...]
    csum_ref[...] = c[tm - 1:tm, :]

    lane = lax.broadcasted_iota(jnp.int32, (1, LANES), 1)
    piece = lane % AUG_TERMS
    c_hi, c_lo, c_lo2 = _split3(c * LOG2_E)
    c3 = jnp.where(piece == 0, c_hi, jnp.where(piece == 1, c_lo, c_lo2))
    c3_neg = -c3

    heads_per_chunk = MXU_COLS // FOX_DH
    for chunk in range(FOX_WIDTH // MXU_COLS):
        cols = slice(chunk * MXU_COLS, (chunk + 1) * MXU_COLS)
        qc = jnp.dot(h, wqkv_ref[:, cols], preferred_element_type=F32) * (FOX_DH ** -0.5 * LOG2_E)
        kc = jnp.dot(h, wqkv_ref[:, FOX_WIDTH + chunk * MXU_COLS:FOX_WIDTH + (chunk + 1) * MXU_COLS],
                     preferred_element_type=F32)
        for sub_head in range(heads_per_chunk):
            head = chunk * heads_per_chunk + sub_head
            pr, hh = divmod(sub_head, 2)
            base = FOX_DH if hh == 0 else 0
            a = lane - base
            in_aug = (a >= 0) & (a < FOX_DH)
            rq = pltpu.roll(c3, (base - AUG_TERMS * head) % LANES, 1)
            rk = pltpu.roll(c3_neg, (base + AUG_TERMS - AUG_TERMS * head) % LANES, 1)
            q_fill = jnp.where((a >= AUG_TERMS) & (a < 2 * AUG_TERMS), 1.0, 0.0)
            k_fill = jnp.where((a >= 0) & (a < AUG_TERMS), 1.0, 0.0)
            qp = qc[:, pr * LANES:(pr + 1) * LANES]
            kp = kc[:, pr * LANES:(pr + 1) * LANES]
            hs = slice(head * LANES, (head + 1) * LANES)
            qa_ref[0, :, hs] = jnp.where((a >= 0) & (a < AUG_TERMS), rq,
                                         jnp.where(in_aug, q_fill, qp)).astype(BF16)
            ka_ref[0, :, hs] = jnp.where((a >= AUG_TERMS) & (a < 2 * AUG_TERMS), rk,
                                         jnp.where(in_aug, k_fill, kp)).astype(BF16)

    u_ref[0] = jnp.dot(h, wu_ref[...], preferred_element_type=F32)


def _in_proj(x, g, w_u, w_qkv, w_f, b_f, tm, ta):
    B, S, D = x.shape
    row = lambda n: pl.BlockSpec((1, tm, n), lambda b, s: (b, s, 0))
    return pl.pallas_call(
        _in_proj_kernel,
        grid=(B, S // tm),
        in_specs=[row(D), _const_spec((1, D)), _const_spec(w_u.shape), _const_spec(w_qkv.shape),
                  _const_spec(w_f.shape), _const_spec((1, LANES))],
        out_specs=[row(POOL_WIDTH), row(FOX_HEADS * LANES), row(FOX_HEADS * LANES),
                   pl.BlockSpec((1, tm // ta, FOX_HEADS * V_ROWS, ta), lambda b, s: (b, s, 0, 0))],
        out_shape=[jax.ShapeDtypeStruct((B, S, POOL_WIDTH), F32),
                   jax.ShapeDtypeStruct((B, S, FOX_HEADS * LANES), BF16),
                   jax.ShapeDtypeStruct((B, S, FOX_HEADS * LANES), BF16),
                   jax.ShapeDtypeStruct((B, S // ta, FOX_HEADS * V_ROWS, ta), BF16)],
        scratch_shapes=[pltpu.VMEM((1, LANES), F32)],
        compiler_params=_params("arbitrary", "arbitrary"),
        name="in_proj",
    )(x, g, w_u, w_qkv, w_f, b_f)


def _fox_kernel(q_ref, k_ref, vt_ref, o_ref, st_ref, acc_ref):
    t = q_ref.shape[1]
    qi = pl.program_id(1)
    causal_t = (lax.broadcasted_iota(jnp.int32, (t, t), 0) <=
                lax.broadcasted_iota(jnp.int32, (t, t), 1))
    nt_dims = (((1,), (1,)), ((), ()))

    def scores(j, hd):
        rows = pl.ds(pl.multiple_of(j * t, t), t)
        hs = slice(hd * LANES, (hd + 1) * LANES)
        return lax.dot_general(k_ref[0, rows, hs], q_ref[0, :, hs], nt_dims,
                               preferred_element_type=F32)

    def block(j, ms, last):
        pending = [st_ref[hd] for hd in range(SCORE_LOOKAHEAD)]
        new_m = []
        for hd in range(FOX_HEADS):
            st = pending.pop(0)
            ahead = hd + SCORE_LOOKAHEAD
            if ahead < FOX_HEADS:
                pending.append(scores(j, ahead))
            elif not last:
                st_ref[ahead - FOX_HEADS] = scores(j + 1, ahead - FOX_HEADS)
            if last:
                st = jnp.where(causal_t, st, NEG_BIG)
            m_new = jnp.maximum(ms[hd], jnp.max(st, axis=0, keepdims=True))
            alpha = jnp.exp2(ms[hd] - m_new)
            pt = jnp.exp2(st - m_new).astype(BF16)
            pv = jnp.dot(vt_ref[0, j, hd * V_ROWS:(hd + 1) * V_ROWS, :], pt,
                         preferred_element_type=F32)
            new_m.append(m_new)
            acc_ref[hd] = alpha * acc_ref[hd] + pv
        return tuple(new_m)

    for hd in range(SCORE_LOOKAHEAD):
        st_ref[hd] = scores(0, hd)
    acc_ref[...] = jnp.zeros_like(acc_ref)
    ms = tuple(jnp.full((1, t), NEG_BIG, F32) for _ in range(FOX_HEADS))
    ms = lax.fori_loop(0, qi, functools.partial(block, last=False), ms)
    block(qi, ms, last=True)
    o_t = jnp.concatenate([acc_ref[hd, :FOX_DH, :] / acc_ref[hd, FOX_DH:FOX_DH + 1, :]
                           for hd in range(FOX_HEADS)], axis=0)
    o_ref[0] = o_t.T.astype(BF16)


def _fox_attention(qa, ka, vt, t):
    B, S, _ = qa.shape
    return pl.pallas_call(
        _fox_kernel,
        grid=(B, S // t),
        in_specs=[pl.BlockSpec((1, t, FOX_HEADS * LANES), lambda b, i: (b, i, 0)),
                  pl.BlockSpec((1, S, FOX_HEADS * LANES), lambda b, i: (b, 0, 0)),
                  pl.BlockSpec((1, S // t, FOX_HEADS * V_ROWS, t), lambda b, i: (b, 0, 0, 0))],
        out_specs=pl.BlockSpec((1, t, FOX_WIDTH), lambda b, i: (b, i, 0)),
        out_shape=jax.ShapeDtypeStruct((B, S, FOX_WIDTH), BF16),
        scratch_shapes=[pltpu.VMEM((SCORE_LOOKAHEAD, t, t), F32),
                        pltpu.VMEM((FOX_HEADS, V_ROWS, t), F32)],
        compiler_params=_params("arbitrary", "arbitrary"),
        name="fox_attn",
    )(qa, ka, vt)


def _post_kernel(x_ref, u_ref, o_ref, kv_ref, gmix_ref, wg_ref, bg_ref, pw_ref, ps_ref, wpo_ref,
                 wfo_ref, wo_ref, gx_ref, wxq_ref, wxo_ref, x2_ref, ubuf_ref):
    tm = x_ref.shape[1]
    D = x_ref.shape[2]
    si = pl.program_id(1)
    x = x_ref[0]

    @pl.when(si == 0)
    def _():
        ubuf_ref[0:MAX_WINDOW, :] = jnp.zeros((MAX_WINDOW, POOL_WIDTH), F32)

    @pl.when(si > 0)
    def _():
        ubuf_ref[0:MAX_WINDOW, :] = ubuf_ref[tm:tm + MAX_WINDOW, :]

    u = u_ref[0]
    ubuf_ref[MAX_WINDOW:, :] = u
    pos = si * tm + lax.broadcasted_iota(jnp.int32, (tm, 1), 0)
    ys = []
    for g, w in enumerate(POOL_WINDOWS):
        sl = slice(g * POOL_GC, (g + 1) * POOL_GC)
        ug = u[:, sl]
        ws = ug
        for j in range(1, w):
            ws = ws + ubuf_ref[MAX_WINDOW - j:MAX_WINDOW - j + tm, sl]
        cnt = jnp.minimum(pos + 1, w).astype(F32)
        d = (ws / cnt - ug).astype(BF16)
        ys.append(jnp.dot(d, pw_ref[g], preferred_element_type=F32))
    y = (jnp.concatenate(ys, axis=1) * ps_ref[...]).astype(BF16)
    y_pool = jnp.dot(y, wpo_ref[...], preferred_element_type=F32)

    h = _rmsnorm(x, gmix_ref[...]).astype(BF16)
    gates = _sigmoid(jnp.dot(h, wg_ref[...], preferred_element_type=F32) + bg_ref[...])
    y_fox = jnp.dot(o_ref[0], wfo_ref[...], preferred_element_type=F32)
    mix = (gates[:, :D] * y_pool + gates[:, D:] * y_fox).astype(BF16)
    x1 = x + jnp.dot(mix, wo_ref[...], preferred_element_type=F32)

    hx = _rmsnorm(x1, gx_ref[...]).astype(BF16)
    qx = jnp.dot(hx, wxq_ref[...], preferred_element_type=F32).astype(BF16)
    kv = kv_ref[0]
    heads = []
    for hd in range(X_HEADS):
        sl = slice(hd * X_DH, (hd + 1) * X_DH)
        s = lax.dot_general(qx[:, sl], kv[:, sl], (((1,), (1,)), ((), ())),
                            preferred_element_type=F32) * (X_DH ** -0.5)
        p = jnp.exp(s - jnp.max(s, axis=1, keepdims=True))
        l = jnp.sum(p, axis=1, keepdims=True)
        vh = kv[:, X_WIDTH + hd * X_DH:X_WIDTH + (hd + 1) * X_DH]
        heads.append(jnp.dot(p.astype(BF16), vh, preferred_element_type=F32) / l)
    ox = jnp.concatenate(heads, axis=1).astype(BF16)
    x2_ref[0] = x1 + jnp.dot(ox, wxo_ref[...], preferred_element_type=F32)


def _post(x, u, o, kv, gmix, w_g, b_g, pool_w, pool_scale, w_po, w_fo, w_o, gx, w_xq, w_xo, tm):
    B, S, D = x.shape
    M = kv.shape[1]
    row = lambda n: pl.BlockSpec((1, tm, n), lambda b, s: (b, s, 0))
    consts = [gmix, w_g, b_g, pool_w, pool_scale, w_po, w_fo, w_o, gx, w_xq, w_xo]
    return pl.pallas_call(
        _post_kernel,
        grid=(B, S // tm),
        in_specs=[row(D), row(POOL_WIDTH), row(FOX_WIDTH),
                  pl.BlockSpec((1, M, 2 * X_WIDTH), lambda b, s: (b, 0, 0))]
                 + [_const_spec(a.shape) for a in consts],
        out_specs=row(D),
        out_shape=jax.ShapeDtypeStruct((B, S, D), F32),
        scratch_shapes=[pltpu.VMEM((tm + MAX_WINDOW, POOL_WIDTH), F32)],
        compiler_params=_params("arbitrary", "arbitrary"),
        name="post",
    )(x, u, o, kv, *consts)


def _ffn_kernel(x_ref, g_ref, wi_ref, wo_ref, gfin_ref, out_ref, *, final_norm):
    d_ff = wo_ref.shape[0]
    x = x_ref[...]
    hf = _rmsnorm(x, g_ref[...]).astype(BF16)
    gu = jnp.dot(hf, wi_ref[...], preferred_element_type=F32)
    gt = gu[:, :d_ff]
    act = (gt * _sigmoid(gt) * gu[:, d_ff:]).astype(BF16)
    x3 = x + jnp.dot(act, wo_ref[...], preferred_element_type=F32)
    out_ref[...] = _rmsnorm(x3, gfin_ref[...]) if final_norm else x3


def _ffn(x2, g, w_i, w_o, g_fin, tm, final_norm):
    T, D = x2.shape
    return pl.pallas_call(
        functools.partial(_ffn_kernel, final_norm=final_norm),
        grid=(T // tm,),
        in_specs=[pl.BlockSpec((tm, D), lambda i: (i, 0)), _const_spec((1, D)),
                  pl.BlockSpec(w_i.shape, lambda i: (0, 0), pipeline_mode=pl.Buffered(1)),
                  pl.BlockSpec(w_o.shape, lambda i: (0, 0), pipeline_mode=pl.Buffered(1)),
                  _const_spec((1, D))],
        out_specs=pl.BlockSpec((tm, D), lambda i: (i, 0)),
        out_shape=jax.ShapeDtypeStruct((T, D), F32),
        compiler_params=_params("arbitrary"),
        name="ffn",
    )(x2, g, w_i, w_o, g_fin)


def kernel(x, mem, norm_mix_g, w_in, b_forget, b_gate, pool_w, pool_scale, w_pool_out, w_fox_out,
           w_out, norm_x_g, norm_mem_g, w_xq, w_xkv, w_xo, norm_ffn_g, w_ffn_in, w_ffn_out,
           norm_final_g):
    B, S, D = x.shape
    depth = w_in.shape[0]
    qkv_end = POOL_WIDTH + 3 * FOX_WIDTH
    f_end = qkv_end + FOX_HEADS
    row2 = lambda a: a.reshape(1, -1)
    for l in range(depth):
        wl = w_in[l].astype(BF16)
        rep = AUG_TERMS * FOX_HEADS
        w_f = jnp.pad(jnp.repeat(wl[:, qkv_end:f_end], AUG_TERMS, axis=1), ((0, 0), (0, LANES - rep)))
        b_f = jnp.pad(jnp.repeat(b_forget[l], AUG_TERMS), (0, LANES - rep)).reshape(1, LANES)
        kv = _mem_kv(mem, row2(norm_mem_g[l]), w_xkv[l].astype(BF16))
        u, qa, ka, vt = _in_proj(x, row2(norm_mix_g[l]), wl[:, :POOL_WIDTH],
                                 wl[:, POOL_WIDTH:qkv_end], w_f, b_f, ROW_TILE, ATT_TILE)
        o = _fox_attention(qa, ka, vt, ATT_TILE)
        x2 = _post(x, u, o, kv, row2(norm_mix_g[l]), wl[:, f_end:], row2(b_gate[l]),
                   pool_w[l].astype(BF16), row2(pool_scale[l]), w_pool_out[l].astype(BF16),
                   w_fox_out[l].astype(BF16), w_out[l].astype(BF16), row2(norm_x_g[l]),
                   w_xq[l].astype(BF16), w_xo[l].astype(BF16), ROW_TILE)
        x = _ffn(x2.reshape(B * S, D), row2(norm_ffn_g[l]), w_ffn_in[l].astype(BF16),
                 w_ffn_out[l].astype(BF16), row2(norm_final_g), ROW_TILE,
                 final_norm=(l == depth - 1)).reshape(B, S, D)
    return x
```

```python
import functools

import jax
import jax.numpy as jnp
from jax import lax
from jax.experimental import pallas as pl
from jax.experimental.pallas import tpu as pltpu

F32 = jnp.float32
BF16 = jnp.bfloat16

EPS = 1e-6
POOL_WINDOWS = (2, 4, 8, 16)
POOL_GC = 128
POOL_WIDTH = 512
MAX_WINDOW = 16
FOX_HEADS = 8
FOX_DH = 64
FOX_WIDTH = 512
X_HEADS = 4
X_DH = 128
X_WIDTH = 512
LANES = 128
MXU_COLS = 256
VMEM_LIMIT = 56 * 1024 * 1024
NEG_BIG = -1e30
AUG_TERMS = 3
LOG2_E = 1.4426950408889634
V_ROWS = FOX_DH + 16

ROW_TILE = 512
ATT_TILE = 256
SCORE_LOOKAHEAD = 4


def _rmsnorm(x, g):
    return x * lax.rsqrt(jnp.mean(x * x, axis=-1, keepdims=True) + EPS) * g


def _log_sigmoid(x):
    return jnp.minimum(x, 0.0) - jnp.log1p(jnp.exp(-jnp.abs(x)))


def _sigmoid(x):
    return 1.0 / (1.0 + jnp.exp(-x))


def _const_spec(shape):
    return pl.BlockSpec(shape, lambda *_: (0,) * len(shape))


def _params(*sem):
    return pltpu.CompilerParams(dimension_semantics=sem, vmem_limit_bytes=VMEM_LIMIT)


def _mem_kv_kernel(mem_ref, g_ref, w_ref, kv_ref):
    mn = _rmsnorm(mem_ref[0], g_ref[...]).astype(BF16)
    kv_ref[0] = jnp.dot(mn, w_ref[...], preferred_element_type=F32).astype(BF16)


def _mem_kv(mem, g, w_xkv):
    B, M, D = mem.shape
    return pl.pallas_call(
        _mem_kv_kernel,
        grid=(B,),
        in_specs=[pl.BlockSpec((1, M, D), lambda b: (b, 0, 0)),
                  _const_spec((1, D)), _const_spec(w_xkv.shape)],
        out_specs=pl.BlockSpec((1, M, 2 * X_WIDTH), lambda b: (b, 0, 0)),
        out_shape=jax.ShapeDtypeStruct((B, M, 2 * X_WIDTH), BF16),
        compiler_params=_params("arbitrary"),
        name="mem_kv",
    )(mem, g, w_xkv)


def _split3(c):
    hi = c.astype(BF16).astype(F32)
    r = c - hi
    lo = r.astype(BF16).astype(F32)
    return hi, lo, r - lo


def _in_proj_kernel(x_ref, g_ref, wu_ref, wqkv_ref, wf_ref, bf_ref,
                    u_ref, qat_ref, ka_ref, vt_ref, csum_ref):
    tm = x_ref.shape[1]

    @pl.when(pl.program_id(1) == 0)
    def _():
        csum_ref[...] = jnp.zeros_like(csum_ref)

    h = _rmsnorm(x_ref[0], g_ref[...]).astype(BF16)

    logf = _log_sigmoid(jnp.dot(h, wf_ref[...], preferred_element_type=F32) + bf_ref[...])

    ta = vt_ref.shape[3]
    v = jnp.dot(h, wqkv_ref[:, 2 * FOX_WIDTH:], preferred_element_type=F32)
    ones_rows = jnp.where(lax.broadcasted_iota(jnp.int32, (V_ROWS - FOX_DH, ta), 0) == 0,
                          1.0, 0.0).astype(BF16)
    for sub in range(tm // ta):
        v_t = v[sub * ta:(sub + 1) * ta, :].T.astype(BF16)
        for hd in range(FOX_HEADS):
            vt_ref[0, sub, hd * V_ROWS:hd * V_ROWS + FOX_DH, :] = v_t[hd * FOX_DH:(hd + 1) * FOX_DH, :]
            vt_ref[0, sub, hd * V_ROWS + FOX_DH:(hd + 1) * V_ROWS, :] = ones_rows

    row = lax.broadcasted_iota(jnp.int32, (tm, tm), 0)
    col = lax.broadcasted_iota(jnp.int32, (tm, tm), 1)
    tri = jnp.where(row >= col, 1.0, 0.0).astype(BF16)
    parts = jnp.dot(tri, jnp.concatenate(_split3(logf), axis=1).astype(BF16),
                    preferred_element_type=F32)
    c = parts[:, :LANES] + parts[:, LANES:2 * LANES] + parts[:, 2 * LANES:] + csum_ref[...]
    csum_ref[...] = c[tm - 1:tm, :]

    lane = lax.broadcasted_iota(jnp.int32, (1, LANES), 1)
    piece = lane % AUG_TERMS
    c_hi, c_lo, c_lo2 = _split3(c * LOG2_E)
    c3 = jnp.where(piece == 0, c_hi, jnp.where(piece == 1, c_lo, c_lo2))
    c3_neg = -c3

    heads_per_chunk = MXU_COLS // FOX_DH
    for chunk in range(FOX_WIDTH // MXU_COLS):
        cols = slice(chunk * MXU_COLS, (chunk + 1) * MXU_COLS)
        qc = jnp.dot(h, wqkv_ref[:, cols], preferred_element_type=F32) * (FOX_DH ** -0.5 * LOG2_E)
        kc = jnp.dot(h, wqkv_ref[:, FOX_WIDTH + chunk * MXU_COLS:FOX_WIDTH + (chunk + 1) * MXU_COLS],
                     preferred_element_type=F32)
        for sub_head in range(heads_per_chunk):
            head = chunk * heads_per_chunk + sub_head
            pr, hh = divmod(sub_head, 2)
            base = FOX_DH if hh == 0 else 0
            a = lane - base
            in_aug = (a >= 0) & (a < FOX_DH)
            rq = pltpu.roll(c3, (base - AUG_TERMS * head) % LANES, 1)
            rk = pltpu.roll(c3_neg, (base + AUG_TERMS - AUG_TERMS * head) % LANES, 1)
            q_fill = jnp.where((a >= AUG_TERMS) & (a < 2 * AUG_TERMS), 1.0, 0.0)
            k_fill = jnp.where((a >= 0) & (a < AUG_TERMS), 1.0, 0.0)
            qp = qc[:, pr * LANES:(pr + 1) * LANES]
            kp = kc[:, pr * LANES:(pr + 1) * LANES]
            hs = slice(head * LANES, (head + 1) * LANES)
            qat_ref[0, head] = jnp.where((a >= 0) & (a < AUG_TERMS), rq,
                                         jnp.where(in_aug, q_fill, qp)).T.astype(BF16)
            ka_ref[0, :, hs] = jnp.where((a >= AUG_TERMS) & (a < 2 * AUG_TERMS), rk,
                                         jnp.where(in_aug, k_fill, kp)).astype(BF16)

    u_ref[0] = jnp.dot(h, wu_ref[...], preferred_element_type=F32)


def _in_proj(x, g, w_u, w_qkv, w_f, b_f, tm, ta):
    B, S, D = x.shape
    row = lambda n: pl.BlockSpec((1, tm, n), lambda b, s: (b, s, 0))
    return pl.pallas_call(
        _in_proj_kernel,
        grid=(B, S // tm),
        in_specs=[row(D), _const_spec((1, D)), _const_spec(w_u.shape), _const_spec(w_qkv.shape),
                  _const_spec(w_f.shape), _const_spec((1, LANES))],
        out_specs=[row(POOL_WIDTH),
                   pl.BlockSpec((1, FOX_HEADS, LANES, tm), lambda b, s: (b, 0, 0, s)),
                   row(FOX_HEADS * LANES),
                   pl.BlockSpec((1, tm // ta, FOX_HEADS * V_ROWS, ta), lambda b, s: (b, s, 0, 0))],
        out_shape=[jax.ShapeDtypeStruct((B, S, POOL_WIDTH), F32),
                   jax.ShapeDtypeStruct((B, FOX_HEADS, LANES, S), BF16),
                   jax.ShapeDtypeStruct((B, S, FOX_HEADS * LANES), BF16),
                   jax.ShapeDtypeStruct((B, S // ta, FOX_HEADS * V_ROWS, ta), BF16)],
        scratch_shapes=[pltpu.VMEM((1, LANES), F32)],
        compiler_params=_params("arbitrary", "arbitrary"),
        name="in_proj",
    )(x, g, w_u, w_qkv, w_f, b_f)


def _fox_kernel(qt_ref, k_ref, vt_ref, o_ref, st_ref, acc_ref):
    t = vt_ref.shape[3]
    n_tiles = vt_ref.shape[1]
    causal_t = (lax.broadcasted_iota(jnp.int32, (t, t), 0) <=
                lax.broadcasted_iota(jnp.int32, (t, t), 1))

    def scores(qi, j, hd):
        rows = pl.ds(pl.multiple_of(j * t, t), t)
        return jnp.dot(k_ref[0, rows, hd * LANES:(hd + 1) * LANES],
                       qt_ref[0, hd, :, qi * t:(qi + 1) * t],
                       preferred_element_type=F32)

    def block(j, ms, qi, last):
        acc = acc_ref.at[qi % 2]
        pending = [st_ref[hd] for hd in range(SCORE_LOOKAHEAD)]
        new_m = []
        for hd in range(FOX_HEADS):
            st = pending.pop(0)
            ahead = hd + SCORE_LOOKAHEAD
            if ahead < FOX_HEADS:
                pending.append(scores(qi, j, ahead))
            elif not last:
                st_ref[ahead - FOX_HEADS] = scores(qi, j + 1, ahead - FOX_HEADS)
            elif qi + 1 < n_tiles:
                st_ref[ahead - FOX_HEADS] = scores(qi + 1, 0, ahead - FOX_HEADS)
            if last:
                st = jnp.where(causal_t, st, NEG_BIG)
            m_new = jnp.maximum(ms[hd], jnp.max(st, axis=0, keepdims=True))
            alpha = jnp.exp2(ms[hd] - m_new)
            pt = jnp.exp2(st - m_new).astype(BF16)
            pv = jnp.dot(vt_ref[0, j, hd * V_ROWS:(hd + 1) * V_ROWS, :], pt,
                         preferred_element_type=F32)
            new_m.append(m_new)
            acc[hd] = alpha * acc[hd] + pv
        return tuple(new_m)

    for hd in range(SCORE_LOOKAHEAD):
        st_ref[hd] = scores(0, 0, hd)
    for qi in range(n_tiles):
        acc = acc_ref.at[qi % 2]
        acc[...] = jnp.zeros_like(acc)
        ms = tuple(jnp.full((1, t), NEG_BIG, F32) for _ in range(FOX_HEADS))
        ms = lax.fori_loop(0, qi, functools.partial(block, qi=qi, last=False), ms)
        block(qi, ms, qi=qi, last=True)
        o_t = jnp.concatenate([acc[hd, :FOX_DH, :] / acc[hd, FOX_DH:FOX_DH + 1, :]
                               for hd in range(FOX_HEADS)], axis=0)
        o_ref[0, qi * t:(qi + 1) * t, :] = o_t.T.astype(BF16)


def _fox_attention(qat, ka, vt, t):
    B, S, _ = ka.shape
    return pl.pallas_call(
        _fox_kernel,
        grid=(B,),
        in_specs=[pl.BlockSpec((1, FOX_HEADS, LANES, S), lambda b: (b, 0, 0, 0)),
                  pl.BlockSpec((1, S, FOX_HEADS * LANES), lambda b: (b, 0, 0)),
                  pl.BlockSpec((1, S // t, FOX_HEADS * V_ROWS, t), lambda b: (b, 0, 0, 0))],
        out_specs=pl.BlockSpec((1, S, FOX_WIDTH), lambda b: (b, 0, 0)),
        out_shape=jax.ShapeDtypeStruct((B, S, FOX_WIDTH), BF16),
        scratch_shapes=[pltpu.VMEM((SCORE_LOOKAHEAD, t, t), F32),
                        pltpu.VMEM((2, FOX_HEADS, V_ROWS, t), F32)],
        compiler_params=_params("arbitrary"),
        name="fox_attn",
    )(qat, ka, vt)


def _post_kernel(x_ref, u_ref, o_ref, kv_ref, gmix_ref, wg_ref, bg_ref, pw_ref, ps_ref, wpo_ref,
                 wfo_ref, wo_ref, gx_ref, wxq_ref, wxo_ref, x2_ref, ubuf_ref):
    tm = x_ref.shape[1]
    D = x_ref.shape[2]
    si = pl.program_id(1)
    x = x_ref[0]

    @pl.when(si == 0)
    def _():
        ubuf_ref[0:MAX_WINDOW, :] = jnp.zeros((MAX_WINDOW, POOL_WIDTH), F32)

    @pl.when(si > 0)
    def _():
        ubuf_ref[0:MAX_WINDOW, :] = ubuf_ref[tm:tm + MAX_WINDOW, :]

    u = u_ref[0]
    ubuf_ref[MAX_WINDOW:, :] = u
    pos = si * tm + lax.broadcasted_iota(jnp.int32, (tm, 1), 0)
    ys = []
    for g, w in enumerate(POOL_WINDOWS):
        sl = slice(g * POOL_GC, (g + 1) * POOL_GC)
        ug = u[:, sl]
        ws = ug
        for j in range(1, w):
            ws = ws + ubuf_ref[MAX_WINDOW - j:MAX_WINDOW - j + tm, sl]
        cnt = jnp.minimum(pos + 1, w).astype(F32)
        d = (ws / cnt - ug).astype(BF16)
        ys.append(jnp.dot(d, pw_ref[g], preferred_element_type=F32))
    y = (jnp.concatenate(ys, axis=1) * ps_ref[...]).astype(BF16)
    y_pool = jnp.dot(y, wpo_ref[...], preferred_element_type=F32)

    h = _rmsnorm(x, gmix_ref[...]).astype(BF16)
    gates = _sigmoid(jnp.dot(h, wg_ref[...], preferred_element_type=F32) + bg_ref[...])
    y_fox = jnp.dot(o_ref[0], wfo_ref[...], preferred_element_type=F32)
    mix = (gates[:, :D] * y_pool + gates[:, D:] * y_fox).astype(BF16)
    x1 = x + jnp.dot(mix, wo_ref[...], preferred_element_type=F32)

    hx = _rmsnorm(x1, gx_ref[...]).astype(BF16)
    qx = jnp.dot(hx, wxq_ref[...], preferred_element_type=F32).astype(BF16)
    kv = kv_ref[0]
    heads = []
    for hd in range(X_HEADS):
        sl = slice(hd * X_DH, (hd + 1) * X_DH)
        s = lax.dot_general(qx[:, sl], kv[:, sl], (((1,), (1,)), ((), ())),
                            preferred_element_type=F32) * (X_DH ** -0.5)
        p = jnp.exp(s - jnp.max(s, axis=1, keepdims=True))
        l = jnp.sum(p, axis=1, keepdims=True)
        vh = kv[:, X_WIDTH + hd * X_DH:X_WIDTH + (hd + 1) * X_DH]
        heads.append(jnp.dot(p.astype(BF16), vh, preferred_element_type=F32) / l)
    ox = jnp.concatenate(heads, axis=1).astype(BF16)
    x2_ref[0] = x1 + jnp.dot(ox, wxo_ref[...], preferred_element_type=F32)


def _post(x, u, o, kv, gmix, w_g, b_g, pool_w, pool_scale, w_po, w_fo, w_o, gx, w_xq, w_xo, tm):
    B, S, D = x.shape
    M = kv.shape[1]
    row = lambda n: pl.BlockSpec((1, tm, n), lambda b, s: (b, s, 0))
    consts = [gmix, w_g, b_g, pool_w, pool_scale, w_po, w_fo, w_o, gx, w_xq, w_xo]
    return pl.pallas_call(
        _post_kernel,
        grid=(B, S // tm),
        in_specs=[row(D), row(POOL_WIDTH), row(FOX_WIDTH),
                  pl.BlockSpec((1, M, 2 * X_WIDTH), lambda b, s: (b, 0, 0))]
                 + [_const_spec(a.shape) for a in consts],
        out_specs=row(D),
        out_shape=jax.ShapeDtypeStruct((B, S, D), F32),
        scratch_shapes=[pltpu.VMEM((tm + MAX_WINDOW, POOL_WIDTH), F32)],
        compiler_params=_params("arbitrary", "arbitrary"),
        name="post",
    )(x, u, o, kv, *consts)


def _ffn_kernel(x_ref, g_ref, wi_ref, wo_ref, gfin_ref, out_ref, *, final_norm):
    d_ff = wo_ref.shape[0]
    x = x_ref[...]
    hf = _rmsnorm(x, g_ref[...]).astype(BF16)
    gu = jnp.dot(hf, wi_ref[...], preferred_element_type=F32)
    gt = gu[:, :d_ff]
    act = (gt * _sigmoid(gt) * gu[:, d_ff:]).astype(BF16)
    x3 = x + jnp.dot(act, wo_ref[...], preferred_element_type=F32)
    out_ref[...] = _rmsnorm(x3, gfin_ref[...]) if final_norm else x3


def _ffn(x2, g, w_i, w_o, g_fin, tm, final_norm):
    T, D = x2.shape
    return pl.pallas_call(
        functools.partial(_ffn_kernel, final_norm=final_norm),
        grid=(T // tm,),
        in_specs=[pl.BlockSpec((tm, D), lambda i: (i, 0)), _const_spec((1, D)),
                  pl.BlockSpec(w_i.shape, lambda i: (0, 0), pipeline_mode=pl.Buffered(1)),
                  pl.BlockSpec(w_o.shape, lambda i: (0, 0), pipeline_mode=pl.Buffered(1)),
                  _const_spec((1, D))],
        out_specs=pl.BlockSpec((tm, D), lambda i: (i, 0)),
        out_shape=jax.ShapeDtypeStruct((T, D), F32),
        compiler_params=_params("arbitrary"),
        name="ffn",
    )(x2, g, w_i, w_o, g_fin)


def kernel(x, mem, norm_mix_g, w_in, b_forget, b_gate, pool_w, pool_scale, w_pool_out, w_fox_out,
           w_out, norm_x_g, norm_mem_g, w_xq, w_xkv, w_xo, norm_ffn_g, w_ffn_in, w_ffn_out,
           norm_final_g):
    B, S, D = x.shape
    depth = w_in.shape[0]
    qkv_end = POOL_WIDTH + 3 * FOX_WIDTH
    f_end = qkv_end + FOX_HEADS
    row2 = lambda a: a.reshape(1, -1)
    for l in range(depth):
        wl = w_in[l].astype(BF16)
        rep = AUG_TERMS * FOX_HEADS
        w_f = jnp.pad(jnp.repeat(wl[:, qkv_end:f_end], AUG_TERMS, axis=1), ((0, 0), (0, LANES - rep)))
        b_f = jnp.pad(jnp.repeat(b_forget[l], AUG_TERMS), (0, LANES - rep)).reshape(1, LANES)
        kv = _mem_kv(mem, row2(norm_mem_g[l]), w_xkv[l].astype(BF16))
        u, qat, ka, vt = _in_proj(x, row2(norm_mix_g[l]), wl[:, :POOL_WIDTH],
                                 wl[:, POOL_WIDTH:qkv_end], w_f, b_f, ROW_TILE, ATT_TILE)
        o = _fox_attention(qat, ka, vt, ATT_TILE)
        x2 = _post(x, u, o, kv, row2(norm_mix_g[l]), wl[:, f_end:], row2(b_gate[l]),
                   pool_w[l].astype(BF16), row2(pool_scale[l]), w_pool_out[l].astype(BF16),
                   w_fox_out[l].astype(BF16), w_out[l].astype(BF16), row2(norm_x_g[l]),
                   w_xq[l].astype(BF16), w_xo[l].astype(BF16), ROW_TILE)
        x = _ffn(x2.reshape(B * S, D), row2(norm_ffn_g[l]), w_ffn_in[l].astype(BF16),
                 w_ffn_out[l].astype(BF16), row2(norm_final_g), ROW_TILE,
                 final_norm=(l == depth - 1)).reshape(B, S, D)
    return x
```

```python
import functools

import jax
import jax.numpy as jnp
from jax import lax
from jax.experimental import pallas as pl
from jax.experimental.pallas import tpu as pltpu

F32 = jnp.float32
BF16 = jnp.bfloat16

EPS = 1e-6
POOL_WINDOWS = (2, 4, 8, 16)
POOL_GC = 128
POOL_WIDTH = 512
MAX_WINDOW = 16
FOX_HEADS = 8
FOX_DH = 64
FOX_WIDTH = 512
X_HEADS = 4
X_DH = 128
X_WIDTH = 512
LANES = 128
MXU_COLS = 256
VMEM_LIMIT = 56 * 1024 * 1024
NEG_BIG = -1e30
AUG_TERMS = 3
LOG2_E = 1.4426950408889634
V_ROWS = FOX_DH + 16

ROW_TILE = 512
ATT_TILE = 256
SCORE_LOOKAHEAD = 4


def _rmsnorm(x, g):
    return x * lax.rsqrt(jnp.mean(x * x, axis=-1, keepdims=True) + EPS) * g


def _log_sigmoid(x):
    return jnp.minimum(x, 0.0) - jnp.log1p(jnp.exp(-jnp.abs(x)))


def _sigmoid(x):
    return 1.0 / (1.0 + jnp.exp(-x))


def _const_spec(shape):
    return pl.BlockSpec(shape, lambda *_: (0,) * len(shape))


def _params(*sem):
    return pltpu.CompilerParams(dimension_semantics=sem, vmem_limit_bytes=VMEM_LIMIT)


def _mem_kv_kernel(mem_ref, g_ref, w_ref, kv_ref):
    mn = _rmsnorm(mem_ref[0], g_ref[...]).astype(BF16)
    kv_ref[0] = jnp.dot(mn, w_ref[...], preferred_element_type=F32).astype(BF16)


def _mem_kv(mem, g, w_xkv):
    B, M, D = mem.shape
    return pl.pallas_call(
        _mem_kv_kernel,
        grid=(B,),
        in_specs=[pl.BlockSpec((1, M, D), lambda b: (b, 0, 0)),
                  _const_spec((1, D)), _const_spec(w_xkv.shape)],
        out_specs=pl.BlockSpec((1, M, 2 * X_WIDTH), lambda b: (b, 0, 0)),
        out_shape=jax.ShapeDtypeStruct((B, M, 2 * X_WIDTH), BF16),
        compiler_params=_params("arbitrary"),
        name="mem_kv",
    )(mem, g, w_xkv)


def _split3(c):
    hi = c.astype(BF16).astype(F32)
    r = c - hi
    lo = r.astype(BF16).astype(F32)
    return hi, lo, r - lo


def _in_proj_kernel(x_ref, g_ref, wu_ref, wqkv_ref, wf_ref, bf_ref,
                    u_ref, qat_ref, ka_ref, vt_ref, csum_ref):
    tm = x_ref.shape[1]

    @pl.when(pl.program_id(1) == 0)
    def _():
        csum_ref[...] = jnp.zeros_like(csum_ref)

    h = _rmsnorm(x_ref[0], g_ref[...]).astype(BF16)

    logf = _log_sigmoid(jnp.dot(h, wf_ref[...], preferred_element_type=F32) + bf_ref[...])

    ta = vt_ref.shape[3]
    v = jnp.dot(h, wqkv_ref[:, 2 * FOX_WIDTH:], preferred_element_type=F32)
    ones_rows = jnp.where(lax.broadcasted_iota(jnp.int32, (V_ROWS - FOX_DH, ta), 0) == 0,
                          1.0, 0.0).astype(BF16)
    for sub in range(tm // ta):
        v_t = v[sub * ta:(sub + 1) * ta, :].T.astype(BF16)
        for hd in range(FOX_HEADS):
            vt_ref[0, sub, hd * V_ROWS:hd * V_ROWS + FOX_DH, :] = v_t[hd * FOX_DH:(hd + 1) * FOX_DH, :]
            vt_ref[0, sub, hd * V_ROWS + FOX_DH:(hd + 1) * V_ROWS, :] = ones_rows

    row = lax.broadcasted_iota(jnp.int32, (tm, tm), 0)
    col = lax.broadcasted_iota(jnp.int32, (tm, tm), 1)
    tri = jnp.where(row >= col, 1.0, 0.0).astype(BF16)
    parts = jnp.dot(tri, jnp.concatenate(_split3(logf), axis=1).astype(BF16),
                    preferred_element_type=F32)
    c = parts[:, :LANES] + parts[:, LANES:2 * LANES] + parts[:, 2 * LANES:] + csum_ref[...]
    csum_ref[...] = c[tm - 1:tm, :]

    lane = lax.broadcasted_iota(jnp.int32, (1, LANES), 1)
    piece = lane % AUG_TERMS
    c_hi, c_lo, c_lo2 = _split3(c * LOG2_E)
    c3 = jnp.where(piece == 0, c_hi, jnp.where(piece == 1, c_lo, c_lo2))
    c3_neg = -c3

    heads_per_chunk = MXU_COLS // FOX_DH
    for chunk in range(FOX_WIDTH // MXU_COLS):
        cols = slice(chunk * MXU_COLS, (chunk + 1) * MXU_COLS)
        qc = jnp.dot(h, wqkv_ref[:, cols], preferred_element_type=F32) * (FOX_DH ** -0.5 * LOG2_E)
        kc = jnp.dot(h, wqkv_ref[:, FOX_WIDTH + chunk * MXU_COLS:FOX_WIDTH + (chunk + 1) * MXU_COLS],
                     preferred_element_type=F32)
        for sub_head in range(heads_per_chunk):
            head = chunk * heads_per_chunk + sub_head
            pr, hh = divmod(sub_head, 2)
            base = FOX_DH if hh == 0 else 0
            a = lane - base
            in_aug = (a >= 0) & (a < FOX_DH)
            rq = pltpu.roll(c3, (base - AUG_TERMS * head) % LANES, 1)
            rk = pltpu.roll(c3_neg, (base + AUG_TERMS - AUG_TERMS * head) % LANES, 1)
            q_fill = jnp.where((a >= AUG_TERMS) & (a < 2 * AUG_TERMS), 1.0, 0.0)
            k_fill = jnp.where((a >= 0) & (a < AUG_TERMS), 1.0, 0.0)
            qp = qc[:, pr * LANES:(pr + 1) * LANES]
            kp = kc[:, pr * LANES:(pr + 1) * LANES]
            hs = slice(head * LANES, (head + 1) * LANES)
            qat_ref[0, head] = jnp.where((a >= 0) & (a < AUG_TERMS), rq,
                                         jnp.where(in_aug, q_fill, qp)).T.astype(BF16)
            ka_ref[0, :, hs] = jnp.where((a >= AUG_TERMS) & (a < 2 * AUG_TERMS), rk,
                                         jnp.where(in_aug, k_fill, kp)).astype(BF16)

    u_ref[0] = jnp.dot(h, wu_ref[...], preferred_element_type=F32)


def _in_proj(x, g, w_u, w_qkv, w_f, b_f, tm, ta):
    B, S, D = x.shape
    row = lambda n: pl.BlockSpec((1, tm, n), lambda b, s: (b, s, 0))
    return pl.pallas_call(
        _in_proj_kernel,
        grid=(B, S // tm),
        in_specs=[row(D), _const_spec((1, D)), _const_spec(w_u.shape), _const_spec(w_qkv.shape),
                  _const_spec(w_f.shape), _const_spec((1, LANES))],
        out_specs=[row(POOL_WIDTH),
                   pl.BlockSpec((1, FOX_HEADS, LANES, tm), lambda b, s: (b, 0, 0, s)),
                   row(FOX_HEADS * LANES),
                   pl.BlockSpec((1, tm // ta, FOX_HEADS * V_ROWS, ta), lambda b, s: (b, s, 0, 0))],
        out_shape=[jax.ShapeDtypeStruct((B, S, POOL_WIDTH), F32),
                   jax.ShapeDtypeStruct((B, FOX_HEADS, LANES, S), BF16),
                   jax.ShapeDtypeStruct((B, S, FOX_HEADS * LANES), BF16),
                   jax.ShapeDtypeStruct((B, S // ta, FOX_HEADS * V_ROWS, ta), BF16)],
        scratch_shapes=[pltpu.VMEM((1, LANES), F32)],
        compiler_params=_params("arbitrary", "arbitrary"),
        name="in_proj",
    )(x, g, w_u, w_qkv, w_f, b_f)


def _fox_kernel(qt_ref, k_ref, vt_ref, o_ref, st_ref, acc_ref):
    t = vt_ref.shape[3]
    n_tiles = vt_ref.shape[1]
    causal_t = (lax.broadcasted_iota(jnp.int32, (t, t), 0) <=
                lax.broadcasted_iota(jnp.int32, (t, t), 1))

    def scores(qi, j, hd):
        rows = pl.ds(pl.multiple_of(j * t, t), t)
        return jnp.dot(k_ref[0, rows, hd * LANES:(hd + 1) * LANES],
                       qt_ref[0, hd, :, qi * t:(qi + 1) * t],
                       preferred_element_type=F32)

    def block(j, ms, qi, last):
        acc = acc_ref.at[qi % 2]
        pending = [st_ref[hd] for hd in range(SCORE_LOOKAHEAD)]
        new_m = []
        for hd in range(FOX_HEADS):
            st = pending.pop(0)
            ahead = hd + SCORE_LOOKAHEAD
            if ahead < FOX_HEADS:
                pending.append(scores(qi, j, ahead))
            elif not last:
                st_ref[ahead - FOX_HEADS] = scores(qi, j + 1, ahead - FOX_HEADS)
            elif qi + 1 < n_tiles:
                st_ref[ahead - FOX_HEADS] = scores(qi + 1, 0, ahead - FOX_HEADS)
            if last:
                st = jnp.where(causal_t, st, NEG_BIG)
            m_new = jnp.maximum(ms[hd], jnp.max(st, axis=0, keepdims=True))
            alpha = jnp.exp2(ms[hd] - m_new)
            pt = jnp.exp2(st - m_new).astype(BF16)
            pv = jnp.dot(vt_ref[0, j, hd * V_ROWS:(hd + 1) * V_ROWS, :], pt,
                         preferred_element_type=F32)
            new_m.append(m_new)
            acc[hd] = alpha * acc[hd] + pv
        return tuple(new_m)

    for hd in range(SCORE_LOOKAHEAD):
        st_ref[hd] = scores(0, 0, hd)
    for qi in range(n_tiles):
        acc = acc_ref.at[qi % 2]
        acc[...] = jnp.zeros_like(acc)
        ms = tuple(jnp.full((1, t), NEG_BIG, F32) for _ in range(FOX_HEADS))
        for j in range(qi):
            ms = block(j, ms, qi=qi, last=False)
        block(qi, ms, qi=qi, last=True)
        o_t = jnp.concatenate([acc[hd, :FOX_DH, :] / acc[hd, FOX_DH:FOX_DH + 1, :]
                               for hd in range(FOX_HEADS)], axis=0)
        o_ref[0, qi * t:(qi + 1) * t, :] = o_t.T.astype(BF16)


def _fox_attention(qat, ka, vt, t):
    B, S, _ = ka.shape
    return pl.pallas_call(
        _fox_kernel,
        grid=(B,),
        in_specs=[pl.BlockSpec((1, FOX_HEADS, LANES, S), lambda b: (b, 0, 0, 0)),
                  pl.BlockSpec((1, S, FOX_HEADS * LANES), lambda b: (b, 0, 0)),
                  pl.BlockSpec((1, S // t, FOX_HEADS * V_ROWS, t), lambda b: (b, 0, 0, 0))],
        out_specs=pl.BlockSpec((1, S, FOX_WIDTH), lambda b: (b, 0, 0)),
        out_shape=jax.ShapeDtypeStruct((B, S, FOX_WIDTH), BF16),
        scratch_shapes=[pltpu.VMEM((SCORE_LOOKAHEAD, t, t), F32),
                        pltpu.VMEM((2, FOX_HEADS, V_ROWS, t), F32)],
        compiler_params=_params("arbitrary"),
        name="fox_attn",
    )(qat, ka, vt)


def _post_kernel(x_ref, u_ref, o_ref, kv_ref, gmix_ref, wg_ref, bg_ref, pw_ref, ps_ref, wpo_ref,
                 wfo_ref, wo_ref, gx_ref, wxq_ref, wxo_ref, x2_ref, ubuf_ref):
    tm = x_ref.shape[1]
    D = x_ref.shape[2]
    si = pl.program_id(1)
    x = x_ref[0]

    @pl.when(si == 0)
    def _():
        ubuf_ref[0:MAX_WINDOW, :] = jnp.zeros((MAX_WINDOW, POOL_WIDTH), F32)

    @pl.when(si > 0)
    def _():
        ubuf_ref[0:MAX_WINDOW, :] = ubuf_ref[tm:tm + MAX_WINDOW, :]

    u = u_ref[0]
    ubuf_ref[MAX_WINDOW:, :] = u
    pos = si * tm + lax.broadcasted_iota(jnp.int32, (tm, 1), 0)
    ys = []
    for g, w in enumerate(POOL_WINDOWS):
        sl = slice(g * POOL_GC, (g + 1) * POOL_GC)
        ug = u[:, sl]
        ws = ug
        for j in range(1, w):
            ws = ws + ubuf_ref[MAX_WINDOW - j:MAX_WINDOW - j + tm, sl]
        cnt = jnp.minimum(pos + 1, w).astype(F32)
        d = (ws / cnt - ug).astype(BF16)
        ys.append(jnp.dot(d, pw_ref[g], preferred_element_type=F32))
    y = (jnp.concatenate(ys, axis=1) * ps_ref[...]).astype(BF16)
    y_pool = jnp.dot(y, wpo_ref[...], preferred_element_type=F32)

    h = _rmsnorm(x, gmix_ref[...]).astype(BF16)
    gates = _sigmoid(jnp.dot(h, wg_ref[...], preferred_element_type=F32) + bg_ref[...])
    y_fox = jnp.dot(o_ref[0], wfo_ref[...], preferred_element_type=F32)
    mix = (gates[:, :D] * y_pool + gates[:, D:] * y_fox).astype(BF16)
    x1 = x + jnp.dot(mix, wo_ref[...], preferred_element_type=F32)

    hx = _rmsnorm(x1, gx_ref[...]).astype(BF16)
    qx = jnp.dot(hx, wxq_ref[...], preferred_element_type=F32).astype(BF16)
    kv = kv_ref[0]
    heads = []
    for hd in range(X_HEADS):
        sl = slice(hd * X_DH, (hd + 1) * X_DH)
        s = lax.dot_general(qx[:, sl], kv[:, sl], (((1,), (1,)), ((), ())),
                            preferred_element_type=F32) * (X_DH ** -0.5)
        p = jnp.exp(s - jnp.max(s, axis=1, keepdims=True))
        l = jnp.sum(p, axis=1, keepdims=True)
        vh = kv[:, X_WIDTH + hd * X_DH:X_WIDTH + (hd + 1) * X_DH]
        heads.append(jnp.dot(p.astype(BF16), vh, preferred_element_type=F32) / l)
    ox = jnp.concatenate(heads, axis=1).astype(BF16)
    x2_ref[0] = x1 + jnp.dot(ox, wxo_ref[...], preferred_element_type=F32)


def _post(x, u, o, kv, gmix, w_g, b_g, pool_w, pool_scale, w_po, w_fo, w_o, gx, w_xq, w_xo, tm):
    B, S, D = x.shape
    M = kv.shape[1]
    row = lambda n: pl.BlockSpec((1, tm, n), lambda b, s: (b, s, 0))
    consts = [gmix, w_g, b_g, pool_w, pool_scale, w_po, w_fo, w_o, gx, w_xq, w_xo]
    return pl.pallas_call(
        _post_kernel,
        grid=(B, S // tm),
        in_specs=[row(D), row(POOL_WIDTH), row(FOX_WIDTH),
                  pl.BlockSpec((1, M, 2 * X_WIDTH), lambda b, s: (b, 0, 0))]
                 + [_const_spec(a.shape) for a in consts],
        out_specs=row(D),
        out_shape=jax.ShapeDtypeStruct((B, S, D), F32),
        scratch_shapes=[pltpu.VMEM((tm + MAX_WINDOW, POOL_WIDTH), F32)],
        compiler_params=_params("arbitrary", "arbitrary"),
        name="post",
    )(x, u, o, kv, *consts)


def _ffn_kernel(x_ref, g_ref, wi_ref, wo_ref, gfin_ref, out_ref, *, final_norm):
    d_ff = wo_ref.shape[0]
    x = x_ref[...]
    hf = _rmsnorm(x, g_ref[...]).astype(BF16)
    gu = jnp.dot(hf, wi_ref[...], preferred_element_type=F32)
    gt = gu[:, :d_ff]
    act = (gt * _sigmoid(gt) * gu[:, d_ff:]).astype(BF16)
    x3 = x + jnp.dot(act, wo_ref[...], preferred_element_type=F32)
    out_ref[...] = _rmsnorm(x3, gfin_ref[...]) if final_norm else x3


def _ffn(x2, g, w_i, w_o, g_fin, tm, final_norm):
    T, D = x2.shape
    return pl.pallas_call(
        functools.partial(_ffn_kernel, final_norm=final_norm),
        grid=(T // tm,),
        in_specs=[pl.BlockSpec((tm, D), lambda i: (i, 0)), _const_spec((1, D)),
                  pl.BlockSpec(w_i.shape, lambda i: (0, 0), pipeline_mode=pl.Buffered(1)),
                  pl.BlockSpec(w_o.shape, lambda i: (0, 0), pipeline_mode=pl.Buffered(1)),
                  _const_spec((1, D))],
        out_specs=pl.BlockSpec((tm, D), lambda i: (i, 0)),
        out_shape=jax.ShapeDtypeStruct((T, D), F32),
        compiler_params=_params("arbitrary"),
        name="ffn",
    )(x2, g, w_i, w_o, g_fin)


def kernel(x, mem, norm_mix_g, w_in, b_forget, b_gate, pool_w, pool_scale, w_pool_out, w_fox_out,
           w_out, norm_x_g, norm_mem_g, w_xq, w_xkv, w_xo, norm_ffn_g, w_ffn_in, w_ffn_out,
           norm_final_g):
    B, S, D = x.shape
    depth = w_in.shape[0]
    qkv_end = POOL_WIDTH + 3 * FOX_WIDTH
    f_end = qkv_end + FOX_HEADS
    row2 = lambda a: a.reshape(1, -1)
    for l in range(depth):
        wl = w_in[l].astype(BF16)
        rep = AUG_TERMS * FOX_HEADS
        w_f = jnp.pad(jnp.repeat(wl[:, qkv_end:f_end], AUG_TERMS, axis=1), ((0, 0), (0, LANES - rep)))
        b_f = jnp.pad(jnp.repeat(b_forget[l], AUG_TERMS), (0, LANES - rep)).reshape(1, LANES)
        kv = _mem_kv(mem, row2(norm_mem_g[l]), w_xkv[l].astype(BF16))
        u, qat, ka, vt = _in_proj(x, row2(norm_mix_g[l]), wl[:, :POOL_WIDTH],
                                 wl[:, POOL_WIDTH:qkv_end], w_f, b_f, ROW_TILE, ATT_TILE)
        o = _fox_attention(qat, ka, vt, ATT_TILE)
        x2 = _post(x, u, o, kv, row2(norm_mix_g[l]), wl[:, f_end:], row2(b_gate[l]),
                   pool_w[l].astype(BF16), row2(pool_scale[l]), w_pool_out[l].astype(BF16),
                   w_fox_out[l].astype(BF16), w_out[l].astype(BF16), row2(norm_x_g[l]),
                   w_xq[l].astype(BF16), w_xo[l].astype(BF16), ROW_TILE)
        x = _ffn(x2.reshape(B * S, D), row2(norm_ffn_g[l]), w_ffn_in[l].astype(BF16),
                 w_ffn_out[l].astype(BF16), row2(norm_final_g), ROW_TILE,
                 final_norm=(l == depth - 1)).reshape(B, S, D)
    return x
```

```python
import functools

import jax
import jax.numpy as jnp
from jax import lax
from jax.experimental import pallas as pl
from jax.experimental.pallas import tpu as pltpu

F32 = jnp.float32
BF16 = jnp.bfloat16

EPS = 1e-6
POOL_WINDOWS = (2, 4, 8, 16)
POOL_GC = 128
POOL_WIDTH = 512
MAX_WINDOW = 16
FOX_HEADS = 8
FOX_DH = 64
FOX_WIDTH = 512
X_HEADS = 4
X_DH = 128
X_WIDTH = 512
LANES = 128
MXU_COLS = 256
VMEM_LIMIT = 56 * 1024 * 1024
NEG_BIG = -1e30
AUG_TERMS = 3
LOG2_E = 1.4426950408889634
V_ROWS = FOX_DH + 16

ROW_TILE = 512
ATT_TILE = 256
SCORE_LOOKAHEAD = 4


def _rmsnorm(x, g):
    return x * lax.rsqrt(jnp.mean(x * x, axis=-1, keepdims=True) + EPS) * g


def _log_sigmoid(x):
    return jnp.minimum(x, 0.0) - jnp.log1p(jnp.exp(-jnp.abs(x)))


def _sigmoid(x):
    return 1.0 / (1.0 + jnp.exp(-x))


def _const_spec(shape):
    return pl.BlockSpec(shape, lambda *_: (0,) * len(shape))


def _params(*sem):
    return pltpu.CompilerParams(dimension_semantics=sem, vmem_limit_bytes=VMEM_LIMIT)


def _mem_kv_kernel(mem_ref, g_ref, w_ref, kv_ref):
    mn = _rmsnorm(mem_ref[0], g_ref[...]).astype(BF16)
    kv_ref[0] = jnp.dot(mn, w_ref[...], preferred_element_type=F32).astype(BF16)


def _mem_kv(mem, g, w_xkv):
    B, M, D = mem.shape
    return pl.pallas_call(
        _mem_kv_kernel,
        grid=(B,),
        in_specs=[pl.BlockSpec((1, M, D), lambda b: (b, 0, 0)),
                  _const_spec((1, D)), _const_spec(w_xkv.shape)],
        out_specs=pl.BlockSpec((1, M, 2 * X_WIDTH), lambda b: (b, 0, 0)),
        out_shape=jax.ShapeDtypeStruct((B, M, 2 * X_WIDTH), BF16),
        compiler_params=_params("arbitrary"),
        name="mem_kv",
    )(mem, g, w_xkv)


def _split3(c):
    hi = c.astype(BF16).astype(F32)
    r = c - hi
    lo = r.astype(BF16).astype(F32)
    return hi, lo, r - lo


def _in_proj_kernel(x_ref, g_ref, wu_ref, wqkv_ref, wf_ref, bf_ref,
                    u_ref, qat_ref, ka_ref, vt_ref, csum_ref):
    tm = x_ref.shape[1]

    @pl.when(pl.program_id(1) == 0)
    def _():
        csum_ref[...] = jnp.zeros_like(csum_ref)

    h = _rmsnorm(x_ref[0], g_ref[...]).astype(BF16)

    logf = _log_sigmoid(jnp.dot(h, wf_ref[...], preferred_element_type=F32) + bf_ref[...])

    ta = vt_ref.shape[3]
    v = jnp.dot(h, wqkv_ref[:, 2 * FOX_WIDTH:], preferred_element_type=F32)
    ones_rows = jnp.where(lax.broadcasted_iota(jnp.int32, (V_ROWS - FOX_DH, ta), 0) == 0,
                          1.0, 0.0).astype(BF16)
    for sub in range(tm // ta):
        v_t = v[sub * ta:(sub + 1) * ta, :].T.astype(BF16)
        for hd in range(FOX_HEADS):
            vt_ref[0, sub, hd * V_ROWS:hd * V_ROWS + FOX_DH, :] = v_t[hd * FOX_DH:(hd + 1) * FOX_DH, :]
            vt_ref[0, sub, hd * V_ROWS + FOX_DH:(hd + 1) * V_ROWS, :] = ones_rows

    row = lax.broadcasted_iota(jnp.int32, (tm, tm), 0)
    col = lax.broadcasted_iota(jnp.int32, (tm, tm), 1)
    tri = jnp.where(row >= col, 1.0, 0.0).astype(BF16)
    parts = jnp.dot(tri, jnp.concatenate(_split3(logf), axis=1).astype(BF16),
                    preferred_element_type=F32)
    c = parts[:, :LANES] + parts[:, LANES:2 * LANES] + parts[:, 2 * LANES:] + csum_ref[...]
    csum_ref[...] = c[tm - 1:tm, :]

    lane = lax.broadcasted_iota(jnp.int32, (1, LANES), 1)
    piece = lane % AUG_TERMS
    c_hi, c_lo, c_lo2 = _split3(c * LOG2_E)
    c3 = jnp.where(piece == 0, c_hi, jnp.where(piece == 1, c_lo, c_lo2))
    c3_neg = -c3

    heads_per_chunk = MXU_COLS // FOX_DH
    for chunk in range(FOX_WIDTH // MXU_COLS):
        cols = slice(chunk * MXU_COLS, (chunk + 1) * MXU_COLS)
        qc = jnp.dot(h, wqkv_ref[:, cols], preferred_element_type=F32) * (FOX_DH ** -0.5 * LOG2_E)
        kc = jnp.dot(h, wqkv_ref[:, FOX_WIDTH + chunk * MXU_COLS:FOX_WIDTH + (chunk + 1) * MXU_COLS],
                     preferred_element_type=F32)
        for sub_head in range(heads_per_chunk):
            head = chunk * heads_per_chunk + sub_head
            pr, hh = divmod(sub_head, 2)
            base = FOX_DH if hh == 0 else 0
            a = lane - base
            in_aug = (a >= 0) & (a < FOX_DH)
            rq = pltpu.roll(c3, (base - AUG_TERMS * head) % LANES, 1)
            rk = pltpu.roll(c3_neg, (base + AUG_TERMS - AUG_TERMS * head) % LANES, 1)
            q_fill = jnp.where((a >= AUG_TERMS) & (a < 2 * AUG_TERMS), 1.0, 0.0)
            k_fill = jnp.where((a >= 0) & (a < AUG_TERMS), 1.0, 0.0)
            qp = qc[:, pr * LANES:(pr + 1) * LANES]
            kp = kc[:, pr * LANES:(pr + 1) * LANES]
            hs = slice(head * LANES, (head + 1) * LANES)
            qat_ref[0, head] = jnp.where((a >= 0) & (a < AUG_TERMS), rq,
                                         jnp.where(in_aug, q_fill, qp)).T.astype(BF16)
            ka_ref[0, :, hs] = jnp.where((a >= AUG_TERMS) & (a < 2 * AUG_TERMS), rk,
                                         jnp.where(in_aug, k_fill, kp)).astype(BF16)

    u_ref[0] = jnp.dot(h, wu_ref[...], preferred_element_type=F32)


def _in_proj(x, g, w_u, w_qkv, w_f, b_f, tm, ta):
    B, S, D = x.shape
    row = lambda n: pl.BlockSpec((1, tm, n), lambda b, s: (b, s, 0))
    return pl.pallas_call(
        _in_proj_kernel,
        grid=(B, S // tm),
        in_specs=[row(D), _const_spec((1, D)), _const_spec(w_u.shape), _const_spec(w_qkv.shape),
                  _const_spec(w_f.shape), _const_spec((1, LANES))],
        out_specs=[row(POOL_WIDTH),
                   pl.BlockSpec((1, FOX_HEADS, LANES, tm), lambda b, s: (b, 0, 0, s)),
                   row(FOX_HEADS * LANES),
                   pl.BlockSpec((1, tm // ta, FOX_HEADS * V_ROWS, ta), lambda b, s: (b, s, 0, 0))],
        out_shape=[jax.ShapeDtypeStruct((B, S, POOL_WIDTH), F32),
                   jax.ShapeDtypeStruct((B, FOX_HEADS, LANES, S), BF16),
                   jax.ShapeDtypeStruct((B, S, FOX_HEADS * LANES), BF16),
                   jax.ShapeDtypeStruct((B, S // ta, FOX_HEADS * V_ROWS, ta), BF16)],
        scratch_shapes=[pltpu.VMEM((1, LANES), F32)],
        compiler_params=_params("arbitrary", "arbitrary"),
        name="in_proj",
    )(x, g, w_u, w_qkv, w_f, b_f)


def _fox_kernel(qt_ref, k_ref, vt_ref, o_ref, st_ref, acc_ref):
    t = vt_ref.shape[3]
    n_tiles = vt_ref.shape[1]
    causal_t = (lax.broadcasted_iota(jnp.int32, (t, t), 0) <=
                lax.broadcasted_iota(jnp.int32, (t, t), 1))

    def scores(qi, j, hd):
        rows = pl.ds(pl.multiple_of(j * t, t), t)
        return jnp.dot(k_ref[0, rows, hd * LANES:(hd + 1) * LANES],
                       qt_ref[0, hd, :, qi * t:(qi + 1) * t],
                       preferred_element_type=F32)

    def block(j, ms, qi, last):
        acc = acc_ref.at[qi % 2]
        pending = [st_ref[hd] for hd in range(SCORE_LOOKAHEAD)]
        new_m = []
        for hd in range(FOX_HEADS):
            st = pending.pop(0)
            ahead = hd + SCORE_LOOKAHEAD
            if ahead < FOX_HEADS:
                pending.append(scores(qi, j, ahead))
            elif not last:
                st_ref[ahead - FOX_HEADS] = scores(qi, j + 1, ahead - FOX_HEADS)
            elif qi + 1 < n_tiles:
                st_ref[ahead - FOX_HEADS] = scores(qi + 1, 0, ahead - FOX_HEADS)
            if last:
                st = jnp.where(causal_t, st, NEG_BIG)
            m_new = jnp.maximum(ms[hd], jnp.max(st, axis=0, keepdims=True))
            alpha = jnp.exp2(ms[hd] - m_new)
            pt = jnp.exp2(st - m_new).astype(BF16)
            pv = jnp.dot(vt_ref[0, j, hd * V_ROWS:(hd + 1) * V_ROWS, :], pt,
                         preferred_element_type=F32)
            new_m.append(m_new)
            acc[hd] = alpha * acc[hd] + pv
        return tuple(new_m)

    for hd in range(SCORE_LOOKAHEAD):
        st_ref[hd] = scores(0, 0, hd)
    for qi in range(n_tiles):
        acc = acc_ref.at[qi % 2]
        acc[...] = jnp.zeros_like(acc)
        ms = tuple(jnp.full((1, t), NEG_BIG, F32) for _ in range(FOX_HEADS))
        for j in range(qi):
            ms = block(j, ms, qi=qi, last=False)
        block(qi, ms, qi=qi, last=True)
        o_t = jnp.concatenate([acc[hd, :FOX_DH, :] / acc[hd, FOX_DH:FOX_DH + 1, :]
                               for hd in range(FOX_HEADS)], axis=0)
        o_ref[0, qi * t:(qi + 1) * t, :] = o_t.T.astype(BF16)


def _fox_attention(qat, ka, vt, t):
    B, S, _ = ka.shape
    return pl.pallas_call(
        _fox_kernel,
        grid=(B,),
        in_specs=[pl.BlockSpec((1, FOX_HEADS, LANES, S), lambda b: (b, 0, 0, 0)),
                  pl.BlockSpec((1, S, FOX_HEADS * LANES), lambda b: (b, 0, 0)),
                  pl.BlockSpec((1, S // t, FOX_HEADS * V_ROWS, t), lambda b: (b, 0, 0, 0))],
        out_specs=pl.BlockSpec((1, S, FOX_WIDTH), lambda b: (b, 0, 0)),
        out_shape=jax.ShapeDtypeStruct((B, S, FOX_WIDTH), BF16),
        scratch_shapes=[pltpu.VMEM((SCORE_LOOKAHEAD, t, t), F32),
                        pltpu.VMEM((2, FOX_HEADS, V_ROWS, t), F32)],
        compiler_params=_params("arbitrary"),
        name="fox_attn",
    )(qat, ka, vt)


def _post_kernel(x_ref, u_ref, o_ref, kv_ref, gmix_ref, wg_ref, bg_ref, pw_ref, ps_ref, wpo_ref,
                 wfo_ref, wo_ref, gx_ref, wxq_ref, wxo_ref, x2_ref, ubuf_ref):
    tm = x_ref.shape[1]
    D = x_ref.shape[2]
    si = pl.program_id(1)
    x = x_ref[0]

    @pl.when(si == 0)
    def _():
        ubuf_ref[0:MAX_WINDOW, :] = jnp.zeros((MAX_WINDOW, POOL_WIDTH), F32)

    @pl.when(si > 0)
    def _():
        ubuf_ref[0:MAX_WINDOW, :] = ubuf_ref[tm:tm + MAX_WINDOW, :]

    u = u_ref[0]
    ubuf_ref[MAX_WINDOW:, :] = u
    pos = si * tm + lax.broadcasted_iota(jnp.int32, (tm, 1), 0)
    ys = []
    for g, w in enumerate(POOL_WINDOWS):
        sl = slice(g * POOL_GC, (g + 1) * POOL_GC)
        ws = ubuf_ref[:, sl]
        k = 1
        while k < w:
            ws = ws + pltpu.roll(ws, k, 0)
            k *= 2
        cnt = jnp.minimum(pos + 1, w).astype(F32)
        d = (ws[MAX_WINDOW:, :] / cnt - u[:, sl]).astype(BF16)
        ys.append(jnp.dot(d, pw_ref[g], preferred_element_type=F32))
    y = (jnp.concatenate(ys, axis=1) * ps_ref[...]).astype(BF16)
    y_pool = jnp.dot(y, wpo_ref[...], preferred_element_type=F32)

    h = _rmsnorm(x, gmix_ref[...]).astype(BF16)
    gates = _sigmoid(jnp.dot(h, wg_ref[...], preferred_element_type=F32) + bg_ref[...])
    y_fox = jnp.dot(o_ref[0], wfo_ref[...], preferred_element_type=F32)
    mix = (gates[:, :D] * y_pool + gates[:, D:] * y_fox).astype(BF16)
    x1 = x + jnp.dot(mix, wo_ref[...], preferred_element_type=F32)

    hx = _rmsnorm(x1, gx_ref[...]).astype(BF16)
    qx = jnp.dot(hx, wxq_ref[...], preferred_element_type=F32).astype(BF16)
    kv = kv_ref[0]
    heads = []
    for hd in range(X_HEADS):
        sl = slice(hd * X_DH, (hd + 1) * X_DH)
        s = lax.dot_general(qx[:, sl], kv[:, sl], (((1,), (1,)), ((), ())),
                            preferred_element_type=F32) * (X_DH ** -0.5)
        p = jnp.exp(s - jnp.max(s, axis=1, keepdims=True))
        l = jnp.sum(p, axis=1, keepdims=True)
        vh = kv[:, X_WIDTH + hd * X_DH:X_WIDTH + (hd + 1) * X_DH]
        heads.append(jnp.dot(p.astype(BF16), vh, preferred_element_type=F32) / l)
    ox = jnp.concatenate(heads, axis=1).astype(BF16)
    x2_ref[0] = x1 + jnp.dot(ox, wxo_ref[...], preferred_element_type=F32)


def _post(x, u, o, kv, gmix, w_g, b_g, pool_w, pool_scale, w_po, w_fo, w_o, gx, w_xq, w_xo, tm):
    B, S, D = x.shape
    M = kv.shape[1]
    row = lambda n: pl.BlockSpec((1, tm, n), lambda b, s: (b, s, 0))
    consts = [gmix, w_g, b_g, pool_w, pool_scale, w_po, w_fo, w_o, gx, w_xq, w_xo]
    return pl.pallas_call(
        _post_kernel,
        grid=(B, S // tm),
        in_specs=[row(D), row(POOL_WIDTH), row(FOX_WIDTH),
                  pl.BlockSpec((1, M, 2 * X_WIDTH), lambda b, s: (b, 0, 0))]
                 + [_const_spec(a.shape) for a in consts],
        out_specs=row(D),
        out_shape=jax.ShapeDtypeStruct((B, S, D), F32),
        scratch_shapes=[pltpu.VMEM((tm + MAX_WINDOW, POOL_WIDTH), F32)],
        compiler_params=_params("arbitrary", "arbitrary"),
        name="post",
    )(x, u, o, kv, *consts)


def _ffn_kernel(x_ref, g_ref, wi_ref, wo_ref, gfin_ref, out_ref, *, final_norm):
    d_ff = wo_ref.shape[0]
    x = x_ref[...]
    hf = _rmsnorm(x, g_ref[...]).astype(BF16)
    gu = jnp.dot(hf, wi_ref[...], preferred_element_type=F32)
    gt = gu[:, :d_ff]
    act = (gt * _sigmoid(gt) * gu[:, d_ff:]).astype(BF16)
    x3 = x + jnp.dot(act, wo_ref[...], preferred_element_type=F32)
    out_ref[...] = _rmsnorm(x3, gfin_ref[...]) if final_norm else x3


def _ffn(x2, g, w_i, w_o, g_fin, tm, final_norm):
    T, D = x2.shape
    return pl.pallas_call(
        functools.partial(_ffn_kernel, final_norm=final_norm),
        grid=(T // tm,),
        in_specs=[pl.BlockSpec((tm, D), lambda i: (i, 0)), _const_spec((1, D)),
                  pl.BlockSpec(w_i.shape, lambda i: (0, 0), pipeline_mode=pl.Buffered(1)),
                  pl.BlockSpec(w_o.shape, lambda i: (0, 0), pipeline_mode=pl.Buffered(1)),
                  _const_spec((1, D))],
        out_specs=pl.BlockSpec((tm, D), lambda i: (i, 0)),
        out_shape=jax.ShapeDtypeStruct((T, D), F32),
        compiler_params=_params("arbitrary"),
        name="ffn",
    )(x2, g, w_i, w_o, g_fin)


def kernel(x, mem, norm_mix_g, w_in, b_forget, b_gate, pool_w, pool_scale, w_pool_out, w_fox_out,
           w_out, norm_x_g, norm_mem_g, w_xq, w_xkv, w_xo, norm_ffn_g, w_ffn_in, w_ffn_out,
           norm_final_g):
    B, S, D = x.shape
    depth = w_in.shape[0]
    qkv_end = POOL_WIDTH + 3 * FOX_WIDTH
    f_end = qkv_end + FOX_HEADS
    row2 = lambda a: a.reshape(1, -1)
    for l in range(depth):
        wl = w_in[l].astype(BF16)
        rep = AUG_TERMS * FOX_HEADS
        w_f = jnp.pad(jnp.repeat(wl[:, qkv_end:f_end], AUG_TERMS, axis=1), ((0, 0), (0, LANES - rep)))
        b_f = jnp.pad(jnp.repeat(b_forget[l], AUG_TERMS), (0, LANES - rep)).reshape(1, LANES)
        kv = _mem_kv(mem, row2(norm_mem_g[l]), w_xkv[l].astype(BF16))
        u, qat, ka, vt = _in_proj(x, row2(norm_mix_g[l]), wl[:, :POOL_WIDTH],
                                 wl[:, POOL_WIDTH:qkv_end], w_f, b_f, ROW_TILE, ATT_TILE)
        o = _fox_attention(qat, ka, vt, ATT_TILE)
        x2 = _post(x, u, o, kv, row2(norm_mix_g[l]), wl[:, f_end:], row2(b_gate[l]),
                   pool_w[l].astype(BF16), row2(pool_scale[l]), w_pool_out[l].astype(BF16),
                   w_fox_out[l].astype(BF16), w_out[l].astype(BF16), row2(norm_x_g[l]),
                   w_xq[l].astype(BF16), w_xo[l].astype(BF16), ROW_TILE)
        x = _ffn(x2.reshape(B * S, D), row2(norm_ffn_g[l]), w_ffn_in[l].astype(BF16),
                 w_ffn_out[l].astype(BF16), row2(norm_final_g), ROW_TILE,
                 final_norm=(l == depth - 1)).reshape(B, S, D)
    return x
```

```python
import functools

import jax
import jax.numpy as jnp
from jax import lax
from jax.experimental import pallas as pl
from jax.experimental.pallas import tpu as pltpu

F32 = jnp.float32
BF16 = jnp.bfloat16

EPS = 1e-6
POOL_WINDOWS = (2, 4, 8, 16)
POOL_GC = 128
POOL_WIDTH = 512
MAX_WINDOW = 16
FOX_HEADS = 8
FOX_DH = 64
FOX_WIDTH = 512
X_HEADS = 4
X_DH = 128
X_WIDTH = 512
LANES = 128
MXU_COLS = 256
VMEM_LIMIT = 56 * 1024 * 1024
NEG_BIG = -1e30
AUG_TERMS = 3
LOG2_E = 1.4426950408889634
V_ROWS = FOX_DH + 16

ROW_TILE = 512
ATT_TILE = 256
POST_TILE = 1024
POST_STREAMS = 2
SCORE_LOOKAHEAD = 4


def _rmsnorm(x, g):
    return x * lax.rsqrt(jnp.mean(x * x, axis=-1, keepdims=True) + EPS) * g


def _log_sigmoid(x):
    return jnp.minimum(x, 0.0) - jnp.log1p(jnp.exp(-jnp.abs(x)))


def _sigmoid(x):
    return 1.0 / (1.0 + jnp.exp(-x))


def _const_spec(shape):
    return pl.BlockSpec(shape, lambda *_: (0,) * len(shape), pipeline_mode=pl.Buffered(1))


def _params(*sem):
    return pltpu.CompilerParams(dimension_semantics=sem, vmem_limit_bytes=VMEM_LIMIT)


def _mem_kv_kernel(mem_ref, g_ref, w_ref, kv_ref):
    mn = _rmsnorm(mem_ref[0], g_ref[...]).astype(BF16)
    kv_ref[0] = jnp.dot(mn, w_ref[...], preferred_element_type=F32).astype(BF16)


def _mem_kv(mem, g, w_xkv):
    B, M, D = mem.shape
    return pl.pallas_call(
        _mem_kv_kernel,
        grid=(B,),
        in_specs=[pl.BlockSpec((1, M, D), lambda b: (b, 0, 0)),
                  _const_spec((1, D)), _const_spec(w_xkv.shape)],
        out_specs=pl.BlockSpec((1, M, 2 * X_WIDTH), lambda b: (b, 0, 0)),
        out_shape=jax.ShapeDtypeStruct((B, M, 2 * X_WIDTH), BF16),
        compiler_params=_params("arbitrary"),
        name="mem_kv",
    )(mem, g, w_xkv)


def _split3(c):
    hi = c.astype(BF16).astype(F32)
    r = c - hi
    lo = r.astype(BF16).astype(F32)
    return hi, lo, r - lo


def _in_proj_kernel(x_ref, g_ref, wu_ref, wqkv_ref, wf_ref, bf_ref,
                    u_ref, qat_ref, ka_ref, vt_ref, csum_ref):
    tm = x_ref.shape[1]

    @pl.when(pl.program_id(1) == 0)
    def _():
        csum_ref[...] = jnp.zeros_like(csum_ref)

    h = _rmsnorm(x_ref[0], g_ref[...]).astype(BF16)

    logf = _log_sigmoid(jnp.dot(h, wf_ref[...], preferred_element_type=F32) + bf_ref[...])

    ta = vt_ref.shape[3]
    v = jnp.dot(h, wqkv_ref[:, 2 * FOX_WIDTH:], preferred_element_type=F32)
    ones_rows = jnp.where(lax.broadcasted_iota(jnp.int32, (V_ROWS - FOX_DH, ta), 0) == 0,
                          1.0, 0.0).astype(BF16)
    for sub in range(tm // ta):
        v_t = v[sub * ta:(sub + 1) * ta, :].T.astype(BF16)
        for hd in range(FOX_HEADS):
            vt_ref[0, sub, hd * V_ROWS:hd * V_ROWS + FOX_DH, :] = v_t[hd * FOX_DH:(hd + 1) * FOX_DH, :]
            vt_ref[0, sub, hd * V_ROWS + FOX_DH:(hd + 1) * V_ROWS, :] = ones_rows

    row = lax.broadcasted_iota(jnp.int32, (tm, tm), 0)
    col = lax.broadcasted_iota(jnp.int32, (tm, tm), 1)
    tri = jnp.where(row >= col, 1.0, 0.0).astype(BF16)
    parts = jnp.dot(tri, jnp.concatenate(_split3(logf), axis=1).astype(BF16),
                    preferred_element_type=F32)
    c = parts[:, :LANES] + parts[:, LANES:2 * LANES] + parts[:, 2 * LANES:] + csum_ref[...]
    csum_ref[...] = c[tm - 1:tm, :]

    lane = lax.broadcasted_iota(jnp.int32, (1, LANES), 1)
    piece = lane % AUG_TERMS
    c_hi, c_lo, c_lo2 = _split3(c * LOG2_E)
    c3 = jnp.where(piece == 0, c_hi, jnp.where(piece == 1, c_lo, c_lo2))
    c3_neg = -c3

    heads_per_chunk = MXU_COLS // FOX_DH
    for chunk in range(FOX_WIDTH // MXU_COLS):
        cols = slice(chunk * MXU_COLS, (chunk + 1) * MXU_COLS)
        qc = jnp.dot(h, wqkv_ref[:, cols], preferred_element_type=F32) * (FOX_DH ** -0.5 * LOG2_E)
        kc = jnp.dot(h, wqkv_ref[:, FOX_WIDTH + chunk * MXU_COLS:FOX_WIDTH + (chunk + 1) * MXU_COLS],
                     preferred_element_type=F32)
        for sub_head in range(heads_per_chunk):
            head = chunk * heads_per_chunk + sub_head
            pr, hh = divmod(sub_head, 2)
            base = FOX_DH if hh == 0 else 0
            a = lane - base
            in_aug = (a >= 0) & (a < FOX_DH)
            rq = pltpu.roll(c3, (base - AUG_TERMS * head) % LANES, 1)
            rk = pltpu.roll(c3_neg, (base + AUG_TERMS - AUG_TERMS * head) % LANES, 1)
            q_fill = jnp.where((a >= AUG_TERMS) & (a < 2 * AUG_TERMS), 1.0, 0.0)
            k_fill = jnp.where((a >= 0) & (a < AUG_TERMS), 1.0, 0.0)
            qp = qc[:, pr * LANES:(pr + 1) * LANES]
            kp = kc[:, pr * LANES:(pr + 1) * LANES]
            hs = slice(head * LANES, (head + 1) * LANES)
            qat_ref[0, head] = jnp.where((a >= 0) & (a < AUG_TERMS), rq,
                                         jnp.where(in_aug, q_fill, qp)).T.astype(BF16)
            ka_ref[0, :, hs] = jnp.where((a >= AUG_TERMS) & (a < 2 * AUG_TERMS), rk,
                                         jnp.where(in_aug, k_fill, kp)).astype(BF16)

    u_ref[0] = jnp.dot(h, wu_ref[...], preferred_element_type=F32)


def _in_proj(x, g, w_u, w_qkv, w_f, b_f, tm, ta):
    B, S, D = x.shape
    row = lambda n: pl.BlockSpec((1, tm, n), lambda b, s: (b, s, 0))
    return pl.pallas_call(
        _in_proj_kernel,
        grid=(B, S // tm),
        in_specs=[row(D), _const_spec((1, D)), _const_spec(w_u.shape), _const_spec(w_qkv.shape),
                  _const_spec(w_f.shape), _const_spec((1, LANES))],
        out_specs=[row(POOL_WIDTH),
                   pl.BlockSpec((1, FOX_HEADS, LANES, tm), lambda b, s: (b, 0, 0, s)),
                   row(FOX_HEADS * LANES),
                   pl.BlockSpec((1, tm // ta, FOX_HEADS * V_ROWS, ta), lambda b, s: (b, s, 0, 0))],
        out_shape=[jax.ShapeDtypeStruct((B, S, POOL_WIDTH), F32),
                   jax.ShapeDtypeStruct((B, FOX_HEADS, LANES, S), BF16),
                   jax.ShapeDtypeStruct((B, S, FOX_HEADS * LANES), BF16),
                   jax.ShapeDtypeStruct((B, S // ta, FOX_HEADS * V_ROWS, ta), BF16)],
        scratch_shapes=[pltpu.VMEM((1, LANES), F32)],
        compiler_params=_params("arbitrary", "arbitrary"),
        name="in_proj",
    )(x, g, w_u, w_qkv, w_f, b_f)


def _fox_kernel(qt_ref, k_ref, vt_ref, o_ref, st_ref, acc_ref):
    t = vt_ref.shape[3]
    n_tiles = vt_ref.shape[1]
    causal_t = (lax.broadcasted_iota(jnp.int32, (t, t), 0) <=
                lax.broadcasted_iota(jnp.int32, (t, t), 1))

    def scores(qi, j, hd):
        rows = pl.ds(pl.multiple_of(j * t, t), t)
        return jnp.dot(k_ref[0, rows, hd * LANES:(hd + 1) * LANES],
                       qt_ref[0, hd, :, qi * t:(qi + 1) * t],
                       preferred_element_type=F32)

    def block(j, ms, qi, last):
        acc = acc_ref.at[qi % 2]
        pending = [st_ref[hd] for hd in range(SCORE_LOOKAHEAD)]
        new_m = []
        for hd in range(FOX_HEADS):
            st = pending.pop(0)
            ahead = hd + SCORE_LOOKAHEAD
            if ahead < FOX_HEADS:
                pending.append(scores(qi, j, ahead))
            elif not last:
                st_ref[ahead - FOX_HEADS] = scores(qi, j + 1, ahead - FOX_HEADS)
            elif qi + 1 < n_tiles:
                st_ref[ahead - FOX_HEADS] = scores(qi + 1, 0, ahead - FOX_HEADS)
            if last:
                st = jnp.where(causal_t, st, NEG_BIG)
            m_new = jnp.maximum(ms[hd], jnp.max(st, axis=0, keepdims=True))
            alpha = jnp.exp2(ms[hd] - m_new)
            pt = jnp.exp2(st - m_new).astype(BF16)
            pv = jnp.dot(vt_ref[0, j, hd * V_ROWS:(hd + 1) * V_ROWS, :], pt,
                         preferred_element_type=F32)
            new_m.append(m_new)
            acc[hd] = alpha * acc[hd] + pv
        return tuple(new_m)

    for hd in range(SCORE_LOOKAHEAD):
        st_ref[hd] = scores(0, 0, hd)
    for qi in range(n_tiles):
        acc = acc_ref.at[qi % 2]
        acc[...] = jnp.zeros_like(acc)
        ms = tuple(jnp.full((1, t), NEG_BIG, F32) for _ in range(FOX_HEADS))
        for j in range(qi):
            ms = block(j, ms, qi=qi, last=False)
        block(qi, ms, qi=qi, last=True)
        o_t = jnp.concatenate([acc[hd, :FOX_DH, :] / acc[hd, FOX_DH:FOX_DH + 1, :]
                               for hd in range(FOX_HEADS)], axis=0)
        o_ref[0, qi * t:(qi + 1) * t, :] = o_t.T.astype(BF16)


def _fox_attention(qat, ka, vt, t):
    B, S, _ = ka.shape
    return pl.pallas_call(
        _fox_kernel,
        grid=(B,),
        in_specs=[pl.BlockSpec((1, FOX_HEADS, LANES, S), lambda b: (b, 0, 0, 0)),
                  pl.BlockSpec((1, S, FOX_HEADS * LANES), lambda b: (b, 0, 0)),
                  pl.BlockSpec((1, S // t, FOX_HEADS * V_ROWS, t), lambda b: (b, 0, 0, 0))],
        out_specs=pl.BlockSpec((1, S, FOX_WIDTH), lambda b: (b, 0, 0)),
        out_shape=jax.ShapeDtypeStruct((B, S, FOX_WIDTH), BF16),
        scratch_shapes=[pltpu.VMEM((SCORE_LOOKAHEAD, t, t), F32),
                        pltpu.VMEM((2, FOX_HEADS, V_ROWS, t), F32)],
        compiler_params=_params("arbitrary"),
        name="fox_attn",
    )(qat, ka, vt)


def _post_kernel(x_ref, u_ref, o_ref, kv_ref, gmix_ref, wg_ref, bg_ref, pw_ref, ps_ref, wpo_ref,
                 wfo_ref, wo_ref, gx_ref, wxq_ref, wxo_ref, x2_ref, ubuf_ref):
    tm = x_ref.shape[1]
    D = x_ref.shape[2]
    si = pl.program_id(1)
    rows = tm // POST_STREAMS
    streams = [slice(r * rows, (r + 1) * rows) for r in range(POST_STREAMS)]
    nt_dims = (((1,), (1,)), ((), ()))
    dot = functools.partial(jnp.dot, preferred_element_type=F32)

    @pl.when(si == 0)
    def _():
        ubuf_ref[0:MAX_WINDOW, :] = jnp.zeros((MAX_WINDOW, POOL_WIDTH), F32)

    @pl.when(si > 0)
    def _():
        ubuf_ref[0:MAX_WINDOW, :] = ubuf_ref[tm:tm + MAX_WINDOW, :]

    ubuf_ref[MAX_WINDOW:, :] = u_ref[0]

    def pool_delta(r, g):
        w = POOL_WINDOWS[g]
        sl = slice(g * POOL_GC, (g + 1) * POOL_GC)
        ws = ubuf_ref[r.start:r.stop + MAX_WINDOW, sl]
        k = 1
        while k < w:
            ws = ws + pltpu.roll(ws, k, 0)
            k *= 2
        pos = si * tm + r.start + lax.broadcasted_iota(jnp.int32, (rows, 1), 0)
        cnt = jnp.minimum(pos + 1, w).astype(F32)
        return (ws[MAX_WINDOW:, :] / cnt - u_ref[0, r, sl]).astype(BF16)

    y_pools = []
    for r in streams:
        ys = [dot(pool_delta(r, g), pw_ref[g]) for g in range(len(POOL_WINDOWS))]
        y_pools.append(dot((jnp.concatenate(ys, axis=1) * ps_ref[...]).astype(BF16), wpo_ref[...]))

    xs = [x_ref[0, r, :] for r in streams]
    hs = [_rmsnorm(x, gmix_ref[...]).astype(BF16) for x in xs]
    gates = [_sigmoid(dot(h, wg_ref[...]) + bg_ref[...]) for h in hs]
    y_foxs = [dot(o_ref[0, r, :], wfo_ref[...]) for r in streams]
    mixes = [(g[:, :D] * yp + g[:, D:] * yf).astype(BF16)
             for g, yp, yf in zip(gates, y_pools, y_foxs)]
    x1s = [x + dot(m, wo_ref[...]) for x, m in zip(xs, mixes)]

    hxs = [_rmsnorm(x1, gx_ref[...]).astype(BF16) for x1 in x1s]
    qxs = [dot(hx, wxq_ref[...]).astype(BF16) for hx in hxs]
    scores = [[lax.dot_general(qx[:, hd * X_DH:(hd + 1) * X_DH],
                               kv_ref[0, :, hd * X_DH:(hd + 1) * X_DH], nt_dims,
                               preferred_element_type=F32) * (X_DH ** -0.5)
               for hd in range(X_HEADS)] for qx in qxs]
    oxs = []
    for sc in scores:
        heads = []
        for hd, s in enumerate(sc):
            p = jnp.exp(s - jnp.max(s, axis=1, keepdims=True))
            l = jnp.sum(p, axis=1, keepdims=True)
            vh = kv_ref[0, :, X_WIDTH + hd * X_DH:X_WIDTH + (hd + 1) * X_DH]
            heads.append(dot(p.astype(BF16), vh) / l)
        oxs.append(jnp.concatenate(heads, axis=1).astype(BF16))
    for r, x1, ox in zip(streams, x1s, oxs):
        x2_ref[0, r, :] = x1 + dot(ox, wxo_ref[...])


def _post(x, u, o, kv, gmix, w_g, b_g, pool_w, pool_scale, w_po, w_fo, w_o, gx, w_xq, w_xo, tm):
    B, S, D = x.shape
    M = kv.shape[1]
    row = lambda n: pl.BlockSpec((1, tm, n), lambda b, s: (b, s, 0))
    consts = [gmix, w_g, b_g, pool_w, pool_scale, w_po, w_fo, w_o, gx, w_xq, w_xo]
    return pl.pallas_call(
        _post_kernel,
        grid=(B, S // tm),
        in_specs=[row(D), row(POOL_WIDTH), row(FOX_WIDTH),
                  pl.BlockSpec((1, M, 2 * X_WIDTH), lambda b, s: (b, 0, 0))]
                 + [_const_spec(a.shape) for a in consts],
        out_specs=row(D),
        out_shape=jax.ShapeDtypeStruct((B, S, D), F32),
        scratch_shapes=[pltpu.VMEM((tm + MAX_WINDOW, POOL_WIDTH), F32)],
        compiler_params=_params("arbitrary", "arbitrary"),
        name="post",
    )(x, u, o, kv, *consts)


def _ffn_kernel(x_ref, g_ref, wi_ref, wo_ref, gfin_ref, out_ref, *, final_norm):
    d_ff = wo_ref.shape[0]
    x = x_ref[...]
    hf = _rmsnorm(x, g_ref[...]).astype(BF16)
    gu = jnp.dot(hf, wi_ref[...], preferred_element_type=F32)
    gt = gu[:, :d_ff]
    act = (gt * _sigmoid(gt) * gu[:, d_ff:]).astype(BF16)
    x3 = x + jnp.dot(act, wo_ref[...], preferred_element_type=F32)
    out_ref[...] = _rmsnorm(x3, gfin_ref[...]) if final_norm else x3


def _ffn(x2, g, w_i, w_o, g_fin, tm, final_norm):
    T, D = x2.shape
    return pl.pallas_call(
        functools.partial(_ffn_kernel, final_norm=final_norm),
        grid=(T // tm,),
        in_specs=[pl.BlockSpec((tm, D), lambda i: (i, 0)), _const_spec((1, D)),
                  pl.BlockSpec(w_i.shape, lambda i: (0, 0), pipeline_mode=pl.Buffered(1)),
                  pl.BlockSpec(w_o.shape, lambda i: (0, 0), pipeline_mode=pl.Buffered(1)),
                  _const_spec((1, D))],
        out_specs=pl.BlockSpec((tm, D), lambda i: (i, 0)),
        out_shape=jax.ShapeDtypeStruct((T, D), F32),
        compiler_params=_params("arbitrary"),
        name="ffn",
    )(x2, g, w_i, w_o, g_fin)


def kernel(x, mem, norm_mix_g, w_in, b_forget, b_gate, pool_w, pool_scale, w_pool_out, w_fox_out,
           w_out, norm_x_g, norm_mem_g, w_xq, w_xkv, w_xo, norm_ffn_g, w_ffn_in, w_ffn_out,
           norm_final_g):
    B, S, D = x.shape
    depth = w_in.shape[0]
    qkv_end = POOL_WIDTH + 3 * FOX_WIDTH
    f_end = qkv_end + FOX_HEADS
    row2 = lambda a: a.reshape(1, -1)
    for l in range(depth):
        wl = w_in[l].astype(BF16)
        rep = AUG_TERMS * FOX_HEADS
        w_f = jnp.pad(jnp.repeat(wl[:, qkv_end:f_end], AUG_TERMS, axis=1), ((0, 0), (0, LANES - rep)))
        b_f = jnp.pad(jnp.repeat(b_forget[l], AUG_TERMS), (0, LANES - rep)).reshape(1, LANES)
        kv = _mem_kv(mem, row2(norm_mem_g[l]), w_xkv[l].astype(BF16))
        u, qat, ka, vt = _in_proj(x, row2(norm_mix_g[l]), wl[:, :POOL_WIDTH],
                                 wl[:, POOL_WIDTH:qkv_end], w_f, b_f, ROW_TILE, ATT_TILE)
        o = _fox_attention(qat, ka, vt, ATT_TILE)
        x2 = _post(x, u, o, kv, row2(norm_mix_g[l]), wl[:, f_end:], row2(b_gate[l]),
                   pool_w[l].astype(BF16), row2(pool_scale[l]), w_pool_out[l].astype(BF16),
                   w_fox_out[l].astype(BF16), w_out[l].astype(BF16), row2(norm_x_g[l]),
                   w_xq[l].astype(BF16), w_xo[l].astype(BF16), POST_TILE)
        x = _ffn(x2.reshape(B * S, D), row2(norm_ffn_g[l]), w_ffn_in[l].astype(BF16),
                 w_ffn_out[l].astype(BF16), row2(norm_final_g), ROW_TILE,
                 final_norm=(l == depth - 1)).reshape(B, S, D)
    return x
```

```python
import functools

import jax
import jax.numpy as jnp
from jax import lax
from jax.experimental import pallas as pl
from jax.experimental.pallas import tpu as pltpu

F32 = jnp.float32
BF16 = jnp.bfloat16

EPS = 1e-6
POOL_WINDOWS = (2, 4, 8, 16)
POOL_GC = 128
POOL_WIDTH = 512
MAX_WINDOW = 16
FOX_HEADS = 8
FOX_DH = 64
FOX_WIDTH = 512
X_HEADS = 4
X_DH = 128
X_WIDTH = 512
LANES = 128
MXU_COLS = 256
VMEM_LIMIT = 56 * 1024 * 1024
NEG_BIG = -1e30
AUG_TERMS = 3
LOG2_E = 1.4426950408889634
V_ROWS = FOX_DH + 16

ATT_TILE = 256
SCORE_LOOKAHEAD = 4
IN_TILE = 1024
IN_STREAMS = 2
POST_TILE = 1024
POST_STREAMS = 2
FFN_TILE = 1024
FFN_STREAMS = 2
FFN_CHUNK = 1024


def _rmsnorm(x, g):
    return x * lax.rsqrt(jnp.mean(x * x, axis=-1, keepdims=True) + EPS) * g


def _log_sigmoid(x):
    return jnp.minimum(x, 0.0) - jnp.log1p(jnp.exp(-jnp.abs(x)))


def _sigmoid(x):
    return 1.0 / (1.0 + jnp.exp(-x))


def _const_spec(shape):
    return pl.BlockSpec(shape, lambda *_: (0,) * len(shape), pipeline_mode=pl.Buffered(1))


def _params(*sem):
    return pltpu.CompilerParams(dimension_semantics=sem, vmem_limit_bytes=VMEM_LIMIT)


def _mem_kv_kernel(mem_ref, g_ref, w_ref, kv_ref):
    mn = _rmsnorm(mem_ref[0], g_ref[...]).astype(BF16)
    kv_ref[0] = jnp.dot(mn, w_ref[...], preferred_element_type=F32).astype(BF16)


def _mem_kv(mem, g, w_xkv):
    B, M, D = mem.shape
    return pl.pallas_call(
        _mem_kv_kernel,
        grid=(B,),
        in_specs=[pl.BlockSpec((1, M, D), lambda b: (b, 0, 0)),
                  _const_spec((1, D)), _const_spec(w_xkv.shape)],
        out_specs=pl.BlockSpec((1, M, 2 * X_WIDTH), lambda b: (b, 0, 0)),
        out_shape=jax.ShapeDtypeStruct((B, M, 2 * X_WIDTH), BF16),
        compiler_params=_params("arbitrary"),
        name="mem_kv",
    )(mem, g, w_xkv)


def _split3(c):
    hi = c.astype(BF16).astype(F32)
    r = c - hi
    lo = r.astype(BF16).astype(F32)
    return hi, lo, r - lo


def _in_proj_kernel(x_ref, g_ref, wu_ref, wqkv_ref, wf_ref, bf_ref,
                    u_ref, qat_ref, ka_ref, vt_ref, csum_ref):
    tm = x_ref.shape[1]
    rows = tm // IN_STREAMS
    streams = [slice(r * rows, (r + 1) * rows) for r in range(IN_STREAMS)]
    dot = functools.partial(jnp.dot, preferred_element_type=F32)

    @pl.when(pl.program_id(1) == 0)
    def _():
        csum_ref[...] = jnp.zeros_like(csum_ref)

    hs = [_rmsnorm(x_ref[0, r, :], g_ref[...]).astype(BF16) for r in streams]

    logfs = [_log_sigmoid(dot(h, wf_ref[...]) + bf_ref[...]) for h in hs]

    ta = vt_ref.shape[3]
    ones_rows = jnp.where(lax.broadcasted_iota(jnp.int32, (V_ROWS - FOX_DH, ta), 0) == 0,
                          1.0, 0.0).astype(BF16)
    for i, h in enumerate(hs):
        v = dot(h, wqkv_ref[:, 2 * FOX_WIDTH:])
        for sub in range(rows // ta):
            v_t = v[sub * ta:(sub + 1) * ta, :].T.astype(BF16)
            tile = i * (rows // ta) + sub
            for hd in range(FOX_HEADS):
                vt_ref[0, tile, hd * V_ROWS:hd * V_ROWS + FOX_DH, :] = v_t[hd * FOX_DH:(hd + 1) * FOX_DH, :]
                vt_ref[0, tile, hd * V_ROWS + FOX_DH:(hd + 1) * V_ROWS, :] = ones_rows

    row = lax.broadcasted_iota(jnp.int32, (rows, rows), 0)
    col = lax.broadcasted_iota(jnp.int32, (rows, rows), 1)
    tri = jnp.where(row >= col, 1.0, 0.0).astype(BF16)
    parts = [dot(tri, jnp.concatenate(_split3(logf), axis=1).astype(BF16)) for logf in logfs]
    lane = lax.broadcasted_iota(jnp.int32, (1, LANES), 1)
    piece = lane % AUG_TERMS
    c3s = []
    carry = csum_ref[...]
    for p in parts:
        c = p[:, :LANES] + p[:, LANES:2 * LANES] + p[:, 2 * LANES:] + carry
        carry = c[rows - 1:rows, :]
        c_hi, c_lo, c_lo2 = _split3(c * LOG2_E)
        c3s.append(jnp.where(piece == 0, c_hi, jnp.where(piece == 1, c_lo, c_lo2)))
    csum_ref[...] = carry

    heads_per_chunk = MXU_COLS // FOX_DH
    for chunk in range(FOX_WIDTH // MXU_COLS):
        cols = slice(chunk * MXU_COLS, (chunk + 1) * MXU_COLS)
        kcols = slice(FOX_WIDTH + chunk * MXU_COLS, FOX_WIDTH + (chunk + 1) * MXU_COLS)
        qcs = [dot(h, wqkv_ref[:, cols]) * (FOX_DH ** -0.5 * LOG2_E) for h in hs]
        kcs = [dot(h, wqkv_ref[:, kcols]) for h in hs]
        for r, c3, qc, kc in zip(streams, c3s, qcs, kcs):
            c3_neg = -c3
            for sub_head in range(heads_per_chunk):
                head = chunk * heads_per_chunk + sub_head
                pr, hh = divmod(sub_head, 2)
                base = FOX_DH if hh == 0 else 0
                a = lane - base
                in_aug = (a >= 0) & (a < FOX_DH)
                rq = pltpu.roll(c3, (base - AUG_TERMS * head) % LANES, 1)
                rk = pltpu.roll(c3_neg, (base + AUG_TERMS - AUG_TERMS * head) % LANES, 1)
                q_fill = jnp.where((a >= AUG_TERMS) & (a < 2 * AUG_TERMS), 1.0, 0.0)
                k_fill = jnp.where((a >= 0) & (a < AUG_TERMS), 1.0, 0.0)
                qp = qc[:, pr * LANES:(pr + 1) * LANES]
                kp = kc[:, pr * LANES:(pr + 1) * LANES]
                hs_ = slice(head * LANES, (head + 1) * LANES)
                qat_ref[0, head, :, r] = jnp.where((a >= 0) & (a < AUG_TERMS), rq,
                                                   jnp.where(in_aug, q_fill, qp)).T.astype(BF16)
                ka_ref[0, r, hs_] = jnp.where((a >= AUG_TERMS) & (a < 2 * AUG_TERMS), rk,
                                              jnp.where(in_aug, k_fill, kp)).astype(BF16)

    for r, h in zip(streams, hs):
        u_ref[0, r, :] = dot(h, wu_ref[...])


def _in_proj(x, g, w_u, w_qkv, w_f, b_f, tm, ta):
    B, S, D = x.shape
    row = lambda n: pl.BlockSpec((1, tm, n), lambda b, s: (b, s, 0))
    return pl.pallas_call(
        _in_proj_kernel,
        grid=(B, S // tm),
        in_specs=[row(D), _const_spec((1, D)), _const_spec(w_u.shape), _const_spec(w_qkv.shape),
                  _const_spec(w_f.shape), _const_spec((1, LANES))],
        out_specs=[row(POOL_WIDTH),
                   pl.BlockSpec((1, FOX_HEADS, LANES, tm), lambda b, s: (b, 0, 0, s)),
                   row(FOX_HEADS * LANES),
                   pl.BlockSpec((1, tm // ta, FOX_HEADS * V_ROWS, ta), lambda b, s: (b, s, 0, 0))],
        out_shape=[jax.ShapeDtypeStruct((B, S, POOL_WIDTH), F32),
                   jax.ShapeDtypeStruct((B, FOX_HEADS, LANES, S), BF16),
                   jax.ShapeDtypeStruct((B, S, FOX_HEADS * LANES), BF16),
                   jax.ShapeDtypeStruct((B, S // ta, FOX_HEADS * V_ROWS, ta), BF16)],
        scratch_shapes=[pltpu.VMEM((1, LANES), F32)],
        compiler_params=_params("arbitrary", "arbitrary"),
        name="in_proj",
    )(x, g, w_u, w_qkv, w_f, b_f)


def _fox_kernel(qt_ref, k_ref, vt_ref, o_ref, st_ref, acc_ref):
    t = vt_ref.shape[3]
    n_tiles = vt_ref.shape[1]
    causal_t = (lax.broadcasted_iota(jnp.int32, (t, t), 0) <=
                lax.broadcasted_iota(jnp.int32, (t, t), 1))

    def scores(qi, j, hd):
        rows = pl.ds(pl.multiple_of(j * t, t), t)
        return jnp.dot(k_ref[0, rows, hd * LANES:(hd + 1) * LANES],
                       qt_ref[0, hd, :, qi * t:(qi + 1) * t],
                       preferred_element_type=F32)

    def block(j, ms, qi, last):
        acc = acc_ref.at[qi % 2]
        pending = [st_ref[hd] for hd in range(SCORE_LOOKAHEAD)]
        new_m = []
        for hd in range(FOX_HEADS):
            st = pending.pop(0)
            ahead = hd + SCORE_LOOKAHEAD
            if ahead < FOX_HEADS:
                pending.append(scores(qi, j, ahead))
            elif not last:
                st_ref[ahead - FOX_HEADS] = scores(qi, j + 1, ahead - FOX_HEADS)
            elif qi + 1 < n_tiles:
                st_ref[ahead - FOX_HEADS] = scores(qi + 1, 0, ahead - FOX_HEADS)
            if last:
                st = jnp.where(causal_t, st, NEG_BIG)
            m_new = jnp.maximum(ms[hd], jnp.max(st, axis=0, keepdims=True))
            alpha = jnp.exp2(ms[hd] - m_new)
            pt = jnp.exp2(st - m_new).astype(BF16)
            pv = jnp.dot(vt_ref[0, j, hd * V_ROWS:(hd + 1) * V_ROWS, :], pt,
                         preferred_element_type=F32)
            new_m.append(m_new)
            acc[hd] = alpha * acc[hd] + pv
        return tuple(new_m)

    for hd in range(SCORE_LOOKAHEAD):
        st_ref[hd] = scores(0, 0, hd)
    for qi in range(n_tiles):
        acc = acc_ref.at[qi % 2]
        acc[...] = jnp.zeros_like(acc)
        ms = tuple(jnp.full((1, t), NEG_BIG, F32) for _ in range(FOX_HEADS))
        for j in range(qi):
            ms = block(j, ms, qi=qi, last=False)
        block(qi, ms, qi=qi, last=True)
        o_t = jnp.concatenate([acc[hd, :FOX_DH, :] / acc[hd, FOX_DH:FOX_DH + 1, :]
                               for hd in range(FOX_HEADS)], axis=0)
        o_ref[0, qi * t:(qi + 1) * t, :] = o_t.T.astype(BF16)


def _fox_attention(qat, ka, vt, t):
    B, S, _ = ka.shape
    return pl.pallas_call(
        _fox_kernel,
        grid=(B,),
        in_specs=[pl.BlockSpec((1, FOX_HEADS, LANES, S), lambda b: (b, 0, 0, 0)),
                  pl.BlockSpec((1, S, FOX_HEADS * LANES), lambda b: (b, 0, 0)),
                  pl.BlockSpec((1, S // t, FOX_HEADS * V_ROWS, t), lambda b: (b, 0, 0, 0))],
        out_specs=pl.BlockSpec((1, S, FOX_WIDTH), lambda b: (b, 0, 0)),
        out_shape=jax.ShapeDtypeStruct((B, S, FOX_WIDTH), BF16),
        scratch_shapes=[pltpu.VMEM((SCORE_LOOKAHEAD, t, t), F32),
                        pltpu.VMEM((2, FOX_HEADS, V_ROWS, t), F32)],
        compiler_params=_params("arbitrary"),
        name="fox_attn",
    )(qat, ka, vt)


def _post_kernel(x_ref, u_ref, o_ref, kv_ref, gmix_ref, wg_ref, bg_ref, pw_ref, ps_ref, wpo_ref,
                 wfo_ref, wo_ref, gx_ref, wxq_ref, wxo_ref, x2_ref, ubuf_ref):
    tm = x_ref.shape[1]
    D = x_ref.shape[2]
    si = pl.program_id(1)
    rows = tm // POST_STREAMS
    streams = [slice(r * rows, (r + 1) * rows) for r in range(POST_STREAMS)]
    nt_dims = (((1,), (1,)), ((), ()))
    dot = functools.partial(jnp.dot, preferred_element_type=F32)

    @pl.when(si == 0)
    def _():
        ubuf_ref[0:MAX_WINDOW, :] = jnp.zeros((MAX_WINDOW, POOL_WIDTH), F32)

    @pl.when(si > 0)
    def _():
        ubuf_ref[0:MAX_WINDOW, :] = ubuf_ref[tm:tm + MAX_WINDOW, :]

    ubuf_ref[MAX_WINDOW:, :] = u_ref[0]

    def pool_delta(r, g):
        w = POOL_WINDOWS[g]
        sl = slice(g * POOL_GC, (g + 1) * POOL_GC)
        ws = ubuf_ref[r.start:r.stop + MAX_WINDOW, sl]
        k = 1
        while k < w:
            ws = ws + pltpu.roll(ws, k, 0)
            k *= 2
        pos = si * tm + r.start + lax.broadcasted_iota(jnp.int32, (rows, 1), 0)
        cnt = jnp.minimum(pos + 1, w).astype(F32)
        return (ws[MAX_WINDOW:, :] / cnt - u_ref[0, r, sl]).astype(BF16)

    y_pools = []
    for r in streams:
        ys = [dot(pool_delta(r, g), pw_ref[g]) for g in range(len(POOL_WINDOWS))]
        y_pools.append(dot((jnp.concatenate(ys, axis=1) * ps_ref[...]).astype(BF16), wpo_ref[...]))

    xs = [x_ref[0, r, :] for r in streams]
    hs = [_rmsnorm(x, gmix_ref[...]).astype(BF16) for x in xs]
    gates = [_sigmoid(dot(h, wg_ref[...]) + bg_ref[...]) for h in hs]
    y_foxs = [dot(o_ref[0, r, :], wfo_ref[...]) for r in streams]
    mixes = [(g[:, :D] * yp + g[:, D:] * yf).astype(BF16)
             for g, yp, yf in zip(gates, y_pools, y_foxs)]
    x1s = [x + dot(m, wo_ref[...]) for x, m in zip(xs, mixes)]

    hxs = [_rmsnorm(x1, gx_ref[...]).astype(BF16) for x1 in x1s]
    qxs = [dot(hx, wxq_ref[...]).astype(BF16) for hx in hxs]
    scores = [[lax.dot_general(qx[:, hd * X_DH:(hd + 1) * X_DH],
                               kv_ref[0, :, hd * X_DH:(hd + 1) * X_DH], nt_dims,
                               preferred_element_type=F32) * (X_DH ** -0.5)
               for hd in range(X_HEADS)] for qx in qxs]
    oxs = []
    for sc in scores:
        heads = []
        for hd, s in enumerate(sc):
            p = jnp.exp(s - jnp.max(s, axis=1, keepdims=True))
            l = jnp.sum(p, axis=1, keepdims=True)
            vh = kv_ref[0, :, X_WIDTH + hd * X_DH:X_WIDTH + (hd + 1) * X_DH]
            heads.append(dot(p.astype(BF16), vh) / l)
        oxs.append(jnp.concatenate(heads, axis=1).astype(BF16))
    for r, x1, ox in zip(streams, x1s, oxs):
        x2_ref[0, r, :] = x1 + dot(ox, wxo_ref[...])


def _post(x, u, o, kv, gmix, w_g, b_g, pool_w, pool_scale, w_po, w_fo, w_o, gx, w_xq, w_xo, tm):
    B, S, D = x.shape
    M = kv.shape[1]
    row = lambda n: pl.BlockSpec((1, tm, n), lambda b, s: (b, s, 0))
    consts = [gmix, w_g, b_g, pool_w, pool_scale, w_po, w_fo, w_o, gx, w_xq, w_xo]
    return pl.pallas_call(
        _post_kernel,
        grid=(B, S // tm),
        in_specs=[row(D), row(POOL_WIDTH), row(FOX_WIDTH),
                  pl.BlockSpec((1, M, 2 * X_WIDTH), lambda b, s: (b, 0, 0))]
                 + [_const_spec(a.shape) for a in consts],
        out_specs=row(D),
        out_shape=jax.ShapeDtypeStruct((B, S, D), F32),
        scratch_shapes=[pltpu.VMEM((tm + MAX_WINDOW, POOL_WIDTH), F32)],
        compiler_params=_params("arbitrary", "arbitrary"),
        name="post",
    )(x, u, o, kv, *consts)


def _ffn_kernel(x_ref, g_ref, wi_ref, wo_ref, gfin_ref, out_ref, *, final_norm):
    d_ff = wo_ref.shape[0]
    rows = x_ref.shape[0] // FFN_STREAMS
    streams = [slice(r * rows, (r + 1) * rows) for r in range(FFN_STREAMS)]
    dot = functools.partial(jnp.dot, preferred_element_type=F32)
    xs = [x_ref[r, :] for r in streams]
    hfs = [_rmsnorm(x, g_ref[...]).astype(BF16) for x in xs]
    accs = [None] * FFN_STREAMS
    for c0 in range(0, d_ff, FFN_CHUNK):
        c1 = min(c0 + FFN_CHUNK, d_ff)
        gts = [dot(hf, wi_ref[:, c0:c1]) for hf in hfs]
        ups = [dot(hf, wi_ref[:, d_ff + c0:d_ff + c1]) for hf in hfs]
        for i, (gt, up) in enumerate(zip(gts, ups)):
            part = dot((gt * _sigmoid(gt) * up).astype(BF16), wo_ref[c0:c1, :])
            accs[i] = part if accs[i] is None else accs[i] + part
    for r, x, acc in zip(streams, xs, accs):
        x3 = x + acc
        out_ref[r, :] = _rmsnorm(x3, gfin_ref[...]) if final_norm else x3


def _ffn(x2, g, w_i, w_o, g_fin, tm, final_norm):
    T, D = x2.shape
    return pl.pallas_call(
        functools.partial(_ffn_kernel, final_norm=final_norm),
        grid=(T // tm,),
        in_specs=[pl.BlockSpec((tm, D), lambda i: (i, 0)), _const_spec((1, D)),
                  pl.BlockSpec(w_i.shape, lambda i: (0, 0), pipeline_mode=pl.Buffered(1)),
                  pl.BlockSpec(w_o.shape, lambda i: (0, 0), pipeline_mode=pl.Buffered(1)),
                  _const_spec((1, D))],
        out_specs=pl.BlockSpec((tm, D), lambda i: (i, 0)),
        out_shape=jax.ShapeDtypeStruct((T, D), F32),
        compiler_params=_params("arbitrary"),
        name="ffn",
    )(x2, g, w_i, w_o, g_fin)


def kernel(x, mem, norm_mix_g, w_in, b_forget, b_gate, pool_w, pool_scale, w_pool_out, w_fox_out,
           w_out, norm_x_g, norm_mem_g, w_xq, w_xkv, w_xo, norm_ffn_g, w_ffn_in, w_ffn_out,
           norm_final_g):
    B, S, D = x.shape
    depth = w_in.shape[0]
    qkv_end = POOL_WIDTH + 3 * FOX_WIDTH
    f_end = qkv_end + FOX_HEADS
    row2 = lambda a: a.reshape(1, -1)
    for l in range(depth):
        wl = w_in[l].astype(BF16)
        rep = AUG_TERMS * FOX_HEADS
        w_f = jnp.pad(jnp.repeat(wl[:, qkv_end:f_end], AUG_TERMS, axis=1), ((0, 0), (0, LANES - rep)))
        b_f = jnp.pad(jnp.repeat(b_forget[l], AUG_TERMS), (0, LANES - rep)).reshape(1, LANES)
        kv = _mem_kv(mem, row2(norm_mem_g[l]), w_xkv[l].astype(BF16))
        u, qat, ka, vt = _in_proj(x, row2(norm_mix_g[l]), wl[:, :POOL_WIDTH],
                                 wl[:, POOL_WIDTH:qkv_end], w_f, b_f, IN_TILE, ATT_TILE)
        o = _fox_attention(qat, ka, vt, ATT_TILE)
        x2 = _post(x, u, o, kv, row2(norm_mix_g[l]), wl[:, f_end:], row2(b_gate[l]),
                   pool_w[l].astype(BF16), row2(pool_scale[l]), w_pool_out[l].astype(BF16),
                   w_fox_out[l].astype(BF16), w_out[l].astype(BF16), row2(norm_x_g[l]),
                   w_xq[l].astype(BF16), w_xo[l].astype(BF16), POST_TILE)
        x = _ffn(x2.reshape(B * S, D), row2(norm_ffn_g[l]), w_ffn_in[l].astype(BF16),
                 w_ffn_out[l].astype(BF16), row2(norm_final_g), FFN_TILE,
                 final_norm=(l == depth - 1)).reshape(B, S, D)
    return x
```

```python
import functools

import jax
import jax.numpy as jnp
from jax import lax
from jax.experimental import pallas as pl
from jax.experimental.pallas import tpu as pltpu

F32 = jnp.float32
BF16 = jnp.bfloat16

EPS = 1e-6
POOL_WINDOWS = (2, 4, 8, 16)
POOL_GC = 128
POOL_WIDTH = 512
MAX_WINDOW = 16
FOX_HEADS = 8
FOX_DH = 64
FOX_WIDTH = 512
X_HEADS = 4
X_DH = 128
X_WIDTH = 512
LANES = 128
MXU_COLS = 256
VMEM_LIMIT = 56 * 1024 * 1024
NEG_BIG = -1e30
AUG_TERMS = 3
LOG2_E = 1.4426950408889634
V_ROWS = FOX_DH + 16

ATT_TILE = 256
SCORE_LOOKAHEAD = 6
IN_TILE = 1024
IN_STREAMS = 2
POST_TILE = 1024
POST_STREAMS = 2
FFN_TILE = 1024
FFN_STREAMS = 2
FFN_CHUNK = 1024


def _rmsnorm(x, g):
    return x * lax.rsqrt(jnp.mean(x * x, axis=-1, keepdims=True) + EPS) * g


def _log_sigmoid(x):
    return jnp.minimum(x, 0.0) - jnp.log1p(jnp.exp(-jnp.abs(x)))


def _sigmoid(x):
    return 1.0 / (1.0 + jnp.exp(-x))


def _const_spec(shape):
    return pl.BlockSpec(shape, lambda *_: (0,) * len(shape), pipeline_mode=pl.Buffered(1))


def _params(*sem):
    return pltpu.CompilerParams(dimension_semantics=sem, vmem_limit_bytes=VMEM_LIMIT)


def _mem_kv_kernel(mem_ref, g_ref, w_ref, kv_ref):
    mn = _rmsnorm(mem_ref[0], g_ref[...]).astype(BF16)
    kv_ref[0] = jnp.dot(mn, w_ref[...], preferred_element_type=F32).astype(BF16)


def _mem_kv(mem, g, w_xkv):
    B, M, D = mem.shape
    return pl.pallas_call(
        _mem_kv_kernel,
        grid=(B,),
        in_specs=[pl.BlockSpec((1, M, D), lambda b: (b, 0, 0)),
                  _const_spec((1, D)), _const_spec(w_xkv.shape)],
        out_specs=pl.BlockSpec((1, M, 2 * X_WIDTH), lambda b: (b, 0, 0)),
        out_shape=jax.ShapeDtypeStruct((B, M, 2 * X_WIDTH), BF16),
        compiler_params=_params("arbitrary"),
        name="mem_kv",
    )(mem, g, w_xkv)


def _split3(c):
    hi = c.astype(BF16).astype(F32)
    r = c - hi
    lo = r.astype(BF16).astype(F32)
    return hi, lo, r - lo


def _in_proj_kernel(x_ref, g_ref, wu_ref, wqkv_ref, wf_ref, bf_ref,
                    u_ref, qat_ref, ka_ref, vt_ref, csum_ref):
    tm = x_ref.shape[1]
    rows = tm // IN_STREAMS
    streams = [slice(r * rows, (r + 1) * rows) for r in range(IN_STREAMS)]
    dot = functools.partial(jnp.dot, preferred_element_type=F32)

    @pl.when(pl.program_id(1) == 0)
    def _():
        csum_ref[...] = jnp.zeros_like(csum_ref)

    hs = [_rmsnorm(x_ref[0, r, :], g_ref[...]).astype(BF16) for r in streams]

    logfs = [_log_sigmoid(dot(h, wf_ref[...]) + bf_ref[...]) for h in hs]

    ta = vt_ref.shape[3]
    ones_rows = jnp.where(lax.broadcasted_iota(jnp.int32, (V_ROWS - FOX_DH, ta), 0) == 0,
                          1.0, 0.0).astype(BF16)
    for i, h in enumerate(hs):
        v = dot(h, wqkv_ref[:, 2 * FOX_WIDTH:])
        for sub in range(rows // ta):
            v_t = v[sub * ta:(sub + 1) * ta, :].T.astype(BF16)
            tile = i * (rows // ta) + sub
            for hd in range(FOX_HEADS):
                vt_ref[0, tile, hd * V_ROWS:hd * V_ROWS + FOX_DH, :] = v_t[hd * FOX_DH:(hd + 1) * FOX_DH, :]
                vt_ref[0, tile, hd * V_ROWS + FOX_DH:(hd + 1) * V_ROWS, :] = ones_rows

    row = lax.broadcasted_iota(jnp.int32, (rows, rows), 0)
    col = lax.broadcasted_iota(jnp.int32, (rows, rows), 1)
    tri = jnp.where(row >= col, 1.0, 0.0).astype(BF16)
    parts = [dot(tri, jnp.concatenate(_split3(logf), axis=1).astype(BF16)) for logf in logfs]
    lane = lax.broadcasted_iota(jnp.int32, (1, LANES), 1)
    piece = lane % AUG_TERMS
    c3s = []
    carry = csum_ref[...]
    for p in parts:
        c = p[:, :LANES] + p[:, LANES:2 * LANES] + p[:, 2 * LANES:] + carry
        carry = c[rows - 1:rows, :]
        c_hi, c_lo, c_lo2 = _split3(c * LOG2_E)
        c3s.append(jnp.where(piece == 0, c_hi, jnp.where(piece == 1, c_lo, c_lo2)))
    csum_ref[...] = carry

    heads_per_chunk = MXU_COLS // FOX_DH
    for chunk in range(FOX_WIDTH // MXU_COLS):
        cols = slice(chunk * MXU_COLS, (chunk + 1) * MXU_COLS)
        kcols = slice(FOX_WIDTH + chunk * MXU_COLS, FOX_WIDTH + (chunk + 1) * MXU_COLS)
        qcs = [dot(h, wqkv_ref[:, cols]) * (FOX_DH ** -0.5 * LOG2_E) for h in hs]
        kcs = [dot(h, wqkv_ref[:, kcols]) for h in hs]
        for r, c3, qc, kc in zip(streams, c3s, qcs, kcs):
            c3_neg = -c3
            for sub_head in range(heads_per_chunk):
                head = chunk * heads_per_chunk + sub_head
                pr, hh = divmod(sub_head, 2)
                base = FOX_DH if hh == 0 else 0
                a = lane - base
                in_aug = (a >= 0) & (a < FOX_DH)
                rq = pltpu.roll(c3, (base - AUG_TERMS * head) % LANES, 1)
                rk = pltpu.roll(c3_neg, (base + AUG_TERMS - AUG_TERMS * head) % LANES, 1)
                q_fill = jnp.where((a >= AUG_TERMS) & (a < 2 * AUG_TERMS), 1.0, 0.0)
                k_fill = jnp.where((a >= 0) & (a < AUG_TERMS), 1.0, 0.0)
                qp = qc[:, pr * LANES:(pr + 1) * LANES]
                kp = kc[:, pr * LANES:(pr + 1) * LANES]
                hs_ = slice(head * LANES, (head + 1) * LANES)
                qat_ref[0, head, :, r] = jnp.where((a >= 0) & (a < AUG_TERMS), rq,
                                                   jnp.where(in_aug, q_fill, qp)).T.astype(BF16)
                ka_ref[0, r, hs_] = jnp.where((a >= AUG_TERMS) & (a < 2 * AUG_TERMS), rk,
                                              jnp.where(in_aug, k_fill, kp)).astype(BF16)

    for r, h in zip(streams, hs):
        u_ref[0, r, :] = dot(h, wu_ref[...])


def _in_proj(x, g, w_u, w_qkv, w_f, b_f, tm, ta):
    B, S, D = x.shape
    row = lambda n: pl.BlockSpec((1, tm, n), lambda b, s: (b, s, 0))
    return pl.pallas_call(
        _in_proj_kernel,
        grid=(B, S // tm),
        in_specs=[row(D), _const_spec((1, D)), _const_spec(w_u.shape), _const_spec(w_qkv.shape),
                  _const_spec(w_f.shape), _const_spec((1, LANES))],
        out_specs=[row(POOL_WIDTH),
                   pl.BlockSpec((1, FOX_HEADS, LANES, tm), lambda b, s: (b, 0, 0, s)),
                   row(FOX_HEADS * LANES),
                   pl.BlockSpec((1, tm // ta, FOX_HEADS * V_ROWS, ta), lambda b, s: (b, s, 0, 0))],
        out_shape=[jax.ShapeDtypeStruct((B, S, POOL_WIDTH), F32),
                   jax.ShapeDtypeStruct((B, FOX_HEADS, LANES, S), BF16),
                   jax.ShapeDtypeStruct((B, S, FOX_HEADS * LANES), BF16),
                   jax.ShapeDtypeStruct((B, S // ta, FOX_HEADS * V_ROWS, ta), BF16)],
        scratch_shapes=[pltpu.VMEM((1, LANES), F32)],
        compiler_params=_params("arbitrary", "arbitrary"),
        name="in_proj",
    )(x, g, w_u, w_qkv, w_f, b_f)


def _fox_kernel(qt_ref, k_ref, vt_ref, o_ref, st_ref, acc_ref):
    t = vt_ref.shape[3]
    n_tiles = vt_ref.shape[1]
    causal_t = (lax.broadcasted_iota(jnp.int32, (t, t), 0) <=
                lax.broadcasted_iota(jnp.int32, (t, t), 1))

    def scores(qi, j, hd):
        rows = pl.ds(pl.multiple_of(j * t, t), t)
        return jnp.dot(k_ref[0, rows, hd * LANES:(hd + 1) * LANES],
                       qt_ref[0, hd, :, qi * t:(qi + 1) * t],
                       preferred_element_type=F32)

    def block(j, ms, qi, last):
        acc = acc_ref.at[qi % 2]
        pending = [st_ref[hd] for hd in range(SCORE_LOOKAHEAD)]
        new_m = []
        for hd in range(FOX_HEADS):
            st = pending.pop(0)
            ahead = hd + SCORE_LOOKAHEAD
            if ahead < FOX_HEADS:
                pending.append(scores(qi, j, ahead))
            elif not last:
                st_ref[ahead - FOX_HEADS] = scores(qi, j + 1, ahead - FOX_HEADS)
            elif qi + 1 < n_tiles:
                st_ref[ahead - FOX_HEADS] = scores(qi + 1, 0, ahead - FOX_HEADS)
            if last:
                st = jnp.where(causal_t, st, NEG_BIG)
            m_new = jnp.maximum(ms[hd], jnp.max(st, axis=0, keepdims=True))
            alpha = jnp.exp2(ms[hd] - m_new)
            pt = jnp.exp2(st - m_new).astype(BF16)
            pv = jnp.dot(vt_ref[0, j, hd * V_ROWS:(hd + 1) * V_ROWS, :], pt,
                         preferred_element_type=F32)
            new_m.append(m_new)
            acc[hd] = alpha * acc[hd] + pv
        return tuple(new_m)

    for hd in range(SCORE_LOOKAHEAD):
        st_ref[hd] = scores(0, 0, hd)
    for qi in range(n_tiles):
        acc = acc_ref.at[qi % 2]
        acc[...] = jnp.zeros_like(acc)
        ms = tuple(jnp.full((1, t), NEG_BIG, F32) for _ in range(FOX_HEADS))
        for j in range(qi):
            ms = block(j, ms, qi=qi, last=False)
        block(qi, ms, qi=qi, last=True)
        o_t = jnp.concatenate([acc[hd, :FOX_DH, :] / acc[hd, FOX_DH:FOX_DH + 1, :]
                               for hd in range(FOX_HEADS)], axis=0)
        o_ref[0, qi * t:(qi + 1) * t, :] = o_t.T.astype(BF16)


def _fox_attention(qat, ka, vt, t):
    B, S, _ = ka.shape
    return pl.pallas_call(
        _fox_kernel,
        grid=(B,),
        in_specs=[pl.BlockSpec((1, FOX_HEADS, LANES, S), lambda b: (b, 0, 0, 0)),
                  pl.BlockSpec((1, S, FOX_HEADS * LANES), lambda b: (b, 0, 0)),
                  pl.BlockSpec((1, S // t, FOX_HEADS * V_ROWS, t), lambda b: (b, 0, 0, 0))],
        out_specs=pl.BlockSpec((1, S, FOX_WIDTH), lambda b: (b, 0, 0)),
        out_shape=jax.ShapeDtypeStruct((B, S, FOX_WIDTH), BF16),
        scratch_shapes=[pltpu.VMEM((SCORE_LOOKAHEAD, t, t), F32),
                        pltpu.VMEM((2, FOX_HEADS, V_ROWS, t), F32)],
        compiler_params=_params("arbitrary"),
        name="fox_attn",
    )(qat, ka, vt)


def _pool_fold_kernel(pw_ref, ps_ref, wpo_ref, out_ref):
    for g in range(len(POOL_WINDOWS)):
        rows = slice(g * POOL_GC, (g + 1) * POOL_GC)
        scaled = pw_ref[g] * ps_ref[:, rows]
        out_ref[rows, :] = jnp.dot(scaled, wpo_ref[rows, :], preferred_element_type=F32,
                                   precision=lax.Precision.HIGHEST).astype(BF16)


def _pool_fold(pool_w, pool_scale, w_po):
    return pl.pallas_call(
        _pool_fold_kernel,
        out_shape=jax.ShapeDtypeStruct(w_po.shape, BF16),
        name="pool_fold",
    )(pool_w, pool_scale, w_po)


def _post_kernel(x_ref, u_ref, o_ref, kv_ref, gmix_ref, wg_ref, bg_ref, wpool_ref,
                 wfo_ref, wo_ref, gx_ref, wxq_ref, wxo_ref, x2_ref, ubuf_ref):
    tm = x_ref.shape[1]
    D = x_ref.shape[2]
    si = pl.program_id(1)
    rows = tm // POST_STREAMS
    streams = [slice(r * rows, (r + 1) * rows) for r in range(POST_STREAMS)]
    nt_dims = (((1,), (1,)), ((), ()))
    dot = functools.partial(jnp.dot, preferred_element_type=F32)

    @pl.when(si == 0)
    def _():
        ubuf_ref[0:MAX_WINDOW, :] = jnp.zeros((MAX_WINDOW, POOL_WIDTH), F32)

    @pl.when(si > 0)
    def _():
        ubuf_ref[0:MAX_WINDOW, :] = ubuf_ref[tm:tm + MAX_WINDOW, :]

    ubuf_ref[MAX_WINDOW:, :] = u_ref[0]

    def pool_delta(r, g):
        w = POOL_WINDOWS[g]
        sl = slice(g * POOL_GC, (g + 1) * POOL_GC)
        ws = ubuf_ref[r.start:r.stop + MAX_WINDOW, sl]
        k = 1
        while k < w:
            ws = ws + pltpu.roll(ws, k, 0)
            k *= 2
        pos = si * tm + r.start + lax.broadcasted_iota(jnp.int32, (rows, 1), 0)
        cnt = jnp.minimum(pos + 1, w).astype(F32)
        return (ws[MAX_WINDOW:, :] / cnt - u_ref[0, r, sl]).astype(BF16)

    y_pools = [dot(jnp.concatenate([pool_delta(r, g) for g in range(len(POOL_WINDOWS))], axis=1),
                   wpool_ref[...]) for r in streams]

    xs = [x_ref[0, r, :] for r in streams]
    hs = [_rmsnorm(x, gmix_ref[...]).astype(BF16) for x in xs]
    gates = [_sigmoid(dot(h, wg_ref[...]) + bg_ref[...]) for h in hs]
    y_foxs = [dot(o_ref[0, r, :], wfo_ref[...]) for r in streams]
    mixes = [(g[:, :D] * yp + g[:, D:] * yf).astype(BF16)
             for g, yp, yf in zip(gates, y_pools, y_foxs)]
    x1s = [x + dot(m, wo_ref[...]) for x, m in zip(xs, mixes)]

    hxs = [_rmsnorm(x1, gx_ref[...]).astype(BF16) for x1 in x1s]
    qxs = [dot(hx, wxq_ref[...]).astype(BF16) for hx in hxs]
    scores = [[lax.dot_general(qx[:, hd * X_DH:(hd + 1) * X_DH],
                               kv_ref[0, :, hd * X_DH:(hd + 1) * X_DH], nt_dims,
                               preferred_element_type=F32) * (X_DH ** -0.5)
               for hd in range(X_HEADS)] for qx in qxs]
    oxs = []
    for sc in scores:
        heads = []
        for hd, s in enumerate(sc):
            p = jnp.exp(s - jnp.max(s, axis=1, keepdims=True))
            l = jnp.sum(p, axis=1, keepdims=True)
            vh = kv_ref[0, :, X_WIDTH + hd * X_DH:X_WIDTH + (hd + 1) * X_DH]
            heads.append(dot(p.astype(BF16), vh) / l)
        oxs.append(jnp.concatenate(heads, axis=1).astype(BF16))
    for r, x1, ox in zip(streams, x1s, oxs):
        x2_ref[0, r, :] = x1 + dot(ox, wxo_ref[...])


def _post(x, u, o, kv, gmix, w_g, b_g, w_pool, w_fo, w_o, gx, w_xq, w_xo, tm):
    B, S, D = x.shape
    M = kv.shape[1]
    row = lambda n: pl.BlockSpec((1, tm, n), lambda b, s: (b, s, 0))
    consts = [gmix, w_g, b_g, w_pool, w_fo, w_o, gx, w_xq, w_xo]
    return pl.pallas_call(
        _post_kernel,
        grid=(B, S // tm),
        in_specs=[row(D), row(POOL_WIDTH), row(FOX_WIDTH),
                  pl.BlockSpec((1, M, 2 * X_WIDTH), lambda b, s: (b, 0, 0))]
                 + [_const_spec(a.shape) for a in consts],
        out_specs=row(D),
        out_shape=jax.ShapeDtypeStruct((B, S, D), F32),
        scratch_shapes=[pltpu.VMEM((tm + MAX_WINDOW, POOL_WIDTH), F32)],
        compiler_params=_params("arbitrary", "arbitrary"),
        name="post",
    )(x, u, o, kv, *consts)


def _ffn_kernel(x_ref, g_ref, wi_ref, wo_ref, gfin_ref, out_ref, *, final_norm):
    d_ff = wo_ref.shape[0]
    rows = x_ref.shape[0] // FFN_STREAMS
    streams = [slice(r * rows, (r + 1) * rows) for r in range(FFN_STREAMS)]
    dot = functools.partial(jnp.dot, preferred_element_type=F32)
    xs = [x_ref[r, :] for r in streams]
    hfs = [_rmsnorm(x, g_ref[...]).astype(BF16) for x in xs]
    accs = [None] * FFN_STREAMS
    for c0 in range(0, d_ff, FFN_CHUNK):
        c1 = min(c0 + FFN_CHUNK, d_ff)
        gts = [dot(hf, wi_ref[:, c0:c1]) for hf in hfs]
        ups = [dot(hf, wi_ref[:, d_ff + c0:d_ff + c1]) for hf in hfs]
        for i, (gt, up) in enumerate(zip(gts, ups)):
            part = dot((gt * _sigmoid(gt) * up).astype(BF16), wo_ref[c0:c1, :])
            accs[i] = part if accs[i] is None else accs[i] + part
    for r, x, acc in zip(streams, xs, accs):
        x3 = x + acc
        out_ref[r, :] = _rmsnorm(x3, gfin_ref[...]) if final_norm else x3


def _ffn(x2, g, w_i, w_o, g_fin, tm, final_norm):
    T, D = x2.shape
    return pl.pallas_call(
        functools.partial(_ffn_kernel, final_norm=final_norm),
        grid=(T // tm,),
        in_specs=[pl.BlockSpec((tm, D), lambda i: (i, 0)), _const_spec((1, D)),
                  pl.BlockSpec(w_i.shape, lambda i: (0, 0), pipeline_mode=pl.Buffered(1)),
                  pl.BlockSpec(w_o.shape, lambda i: (0, 0), pipeline_mode=pl.Buffered(1)),
                  _const_spec((1, D))],
        out_specs=pl.BlockSpec((tm, D), lambda i: (i, 0)),
        out_shape=jax.ShapeDtypeStruct((T, D), F32),
        compiler_params=_params("arbitrary"),
        name="ffn",
    )(x2, g, w_i, w_o, g_fin)


def kernel(x, mem, norm_mix_g, w_in, b_forget, b_gate, pool_w, pool_scale, w_pool_out, w_fox_out,
           w_out, norm_x_g, norm_mem_g, w_xq, w_xkv, w_xo, norm_ffn_g, w_ffn_in, w_ffn_out,
           norm_final_g):
    B, S, D = x.shape
    depth = w_in.shape[0]
    qkv_end = POOL_WIDTH + 3 * FOX_WIDTH
    f_end = qkv_end + FOX_HEADS
    row2 = lambda a: a.reshape(1, -1)
    for l in range(depth):
        wl = w_in[l]
        rep = AUG_TERMS * FOX_HEADS
        w_f = jnp.pad(jnp.repeat(wl[:, qkv_end:f_end], AUG_TERMS, axis=1),
                      ((0, 0), (0, LANES - rep))).astype(BF16)
        b_f = jnp.pad(jnp.repeat(b_forget[l], AUG_TERMS), (0, LANES - rep)).reshape(1, LANES)
        kv = _mem_kv(mem, row2(norm_mem_g[l]), w_xkv[l].astype(BF16))
        u, qat, ka, vt = _in_proj(x, row2(norm_mix_g[l]), wl[:, :POOL_WIDTH].astype(BF16),
                                 wl[:, POOL_WIDTH:qkv_end].astype(BF16), w_f, b_f, IN_TILE, ATT_TILE)
        o = _fox_attention(qat, ka, vt, ATT_TILE)
        x2 = _post(x, u, o, kv, row2(norm_mix_g[l]), wl[:, f_end:].astype(BF16), row2(b_gate[l]),
                   _pool_fold(pool_w[l], row2(pool_scale[l]), w_pool_out[l]),
                   w_fox_out[l].astype(BF16), w_out[l].astype(BF16), row2(norm_x_g[l]),
                   w_xq[l].astype(BF16), w_xo[l].astype(BF16), POST_TILE)
        x = _ffn(x2.reshape(B * S, D), row2(norm_ffn_g[l]), w_ffn_in[l].astype(BF16),
                 w_ffn_out[l].astype(BF16), row2(norm_final_g), FFN_TILE,
                 final_norm=(l == depth - 1)).reshape(B, S, D)
    return x
```

```python
import functools

import jax
import jax.numpy as jnp
from jax import lax
from jax.experimental import pallas as pl
from jax.experimental.pallas import tpu as pltpu

F32 = jnp.float32
BF16 = jnp.bfloat16

EPS = 1e-6
POOL_WINDOWS = (2, 4, 8, 16)
POOL_GC = 128
POOL_WIDTH = 512
MAX_WINDOW = 16
FOX_HEADS = 8
FOX_DH = 64
FOX_WIDTH = 512
X_HEADS = 4
X_DH = 128
X_WIDTH = 512
LANES = 128
MXU_COLS = 256
VMEM_LIMIT = 56 * 1024 * 1024
NEG_BIG = -1e30
AUG_TERMS = 3
LOG2_E = 1.4426950408889634
V_ROWS = FOX_DH + 16

ATT_TILE = 256
SCORE_LOOKAHEAD = 6
IN_TILE = 1024
IN_STREAMS = 2
POST_TILE = 1024
POST_STREAMS = 2
FFN_TILE = 1024
FFN_STREAMS = 2
FFN_CHUNK = 1024


def _rmsnorm(x, g):
    return x * lax.rsqrt(jnp.mean(x * x, axis=-1, keepdims=True) + EPS) * g


def _log_sigmoid(x):
    return jnp.minimum(x, 0.0) - jnp.log1p(jnp.exp(-jnp.abs(x)))


def _sigmoid(x):
    return 1.0 / (1.0 + jnp.exp(-x))


def _const_spec(shape):
    return pl.BlockSpec(shape, lambda *_: (0,) * len(shape), pipeline_mode=pl.Buffered(1))


def _params(*sem):
    return pltpu.CompilerParams(dimension_semantics=sem, vmem_limit_bytes=VMEM_LIMIT)


def _split3(c):
    hi = c.astype(BF16).astype(F32)
    r = c - hi
    lo = r.astype(BF16).astype(F32)
    return hi, lo, r - lo


def _in_proj_kernel(x_ref, g_ref, wu_ref, wqkv_ref, wf_ref, bf_ref, mem_ref, gmem_ref, wxkv_ref,
                    u_ref, qat_ref, ka_ref, vt_ref, kv_ref, csum_ref):
    tm = x_ref.shape[1]
    rows = tm // IN_STREAMS
    streams = [slice(r * rows, (r + 1) * rows) for r in range(IN_STREAMS)]
    dot = functools.partial(jnp.dot, preferred_element_type=F32)

    @pl.when(pl.program_id(1) == 0)
    def _():
        csum_ref[...] = jnp.zeros_like(csum_ref)
        mem_n = _rmsnorm(mem_ref[0], gmem_ref[...]).astype(BF16)
        kv_ref[0] = dot(mem_n, wxkv_ref[...]).astype(BF16)

    hs = [_rmsnorm(x_ref[0, r, :], g_ref[...]).astype(BF16) for r in streams]

    logfs = [_log_sigmoid(dot(h, wf_ref[...]) + bf_ref[...]) for h in hs]

    ta = vt_ref.shape[3]
    ones_rows = jnp.where(lax.broadcasted_iota(jnp.int32, (V_ROWS - FOX_DH, ta), 0) == 0,
                          1.0, 0.0).astype(BF16)
    for i, h in enumerate(hs):
        v = dot(h, wqkv_ref[:, 2 * FOX_WIDTH:])
        for sub in range(rows // ta):
            v_t = v[sub * ta:(sub + 1) * ta, :].T.astype(BF16)
            tile = i * (rows // ta) + sub
            for hd in range(FOX_HEADS):
                vt_ref[0, tile, hd * V_ROWS:hd * V_ROWS + FOX_DH, :] = v_t[hd * FOX_DH:(hd + 1) * FOX_DH, :]
                vt_ref[0, tile, hd * V_ROWS + FOX_DH:(hd + 1) * V_ROWS, :] = ones_rows

    row = lax.broadcasted_iota(jnp.int32, (rows, rows), 0)
    col = lax.broadcasted_iota(jnp.int32, (rows, rows), 1)
    tri = jnp.where(row >= col, 1.0, 0.0).astype(BF16)
    parts = [dot(tri, jnp.concatenate(_split3(logf), axis=1).astype(BF16)) for logf in logfs]
    lane = lax.broadcasted_iota(jnp.int32, (1, LANES), 1)
    piece = lane % AUG_TERMS
    c3s = []
    carry = csum_ref[...]
    for p in parts:
        c = p[:, :LANES] + p[:, LANES:2 * LANES] + p[:, 2 * LANES:] + carry
        carry = c[rows - 1:rows, :]
        c_hi, c_lo, c_lo2 = _split3(c * LOG2_E)
        c3s.append(jnp.where(piece == 0, c_hi, jnp.where(piece == 1, c_lo, c_lo2)))
    csum_ref[...] = carry

    heads_per_chunk = MXU_COLS // FOX_DH
    for chunk in range(FOX_WIDTH // MXU_COLS):
        cols = slice(chunk * MXU_COLS, (chunk + 1) * MXU_COLS)
        kcols = slice(FOX_WIDTH + chunk * MXU_COLS, FOX_WIDTH + (chunk + 1) * MXU_COLS)
        qcs = [dot(h, wqkv_ref[:, cols]) * (FOX_DH ** -0.5 * LOG2_E) for h in hs]
        kcs = [dot(h, wqkv_ref[:, kcols]) for h in hs]
        for r, c3, qc, kc in zip(streams, c3s, qcs, kcs):
            c3_neg = -c3
            for sub_head in range(heads_per_chunk):
                head = chunk * heads_per_chunk + sub_head
                pr, hh = divmod(sub_head, 2)
                base = FOX_DH if hh == 0 else 0
                a = lane - base
                in_aug = (a >= 0) & (a < FOX_DH)
                rq = pltpu.roll(c3, (base - AUG_TERMS * head) % LANES, 1)
                rk = pltpu.roll(c3_neg, (base + AUG_TERMS - AUG_TERMS * head) % LANES, 1)
                q_fill = jnp.where((a >= AUG_TERMS) & (a < 2 * AUG_TERMS), 1.0, 0.0)
                k_fill = jnp.where((a >= 0) & (a < AUG_TERMS), 1.0, 0.0)
                qp = qc[:, pr * LANES:(pr + 1) * LANES]
                kp = kc[:, pr * LANES:(pr + 1) * LANES]
                hs_ = slice(head * LANES, (head + 1) * LANES)
                qat_ref[0, head, :, r] = jnp.where((a >= 0) & (a < AUG_TERMS), rq,
                                                   jnp.where(in_aug, q_fill, qp)).T.astype(BF16)
                ka_ref[0, r, hs_] = jnp.where((a >= AUG_TERMS) & (a < 2 * AUG_TERMS), rk,
                                              jnp.where(in_aug, k_fill, kp)).astype(BF16)

    for r, h in zip(streams, hs):
        u_ref[0, r, :] = dot(h, wu_ref[...])


def _in_proj(x, g, w_u, w_qkv, w_f, b_f, mem, g_mem, w_xkv, tm, ta):
    B, S, D = x.shape
    M = mem.shape[1]
    row = lambda n: pl.BlockSpec((1, tm, n), lambda b, s: (b, s, 0))
    per_seq = lambda n: pl.BlockSpec((1, M, n), lambda b, s: (b, 0, 0))
    return pl.pallas_call(
        _in_proj_kernel,
        grid=(B, S // tm),
        in_specs=[row(D), _const_spec((1, D)), _const_spec(w_u.shape), _const_spec(w_qkv.shape),
                  _const_spec(w_f.shape), _const_spec((1, LANES)),
                  per_seq(D), _const_spec((1, D)), _const_spec(w_xkv.shape)],
        out_specs=[row(POOL_WIDTH),
                   pl.BlockSpec((1, FOX_HEADS, LANES, tm), lambda b, s: (b, 0, 0, s)),
                   row(FOX_HEADS * LANES),
                   pl.BlockSpec((1, tm // ta, FOX_HEADS * V_ROWS, ta), lambda b, s: (b, s, 0, 0)),
                   per_seq(2 * X_WIDTH)],
        out_shape=[jax.ShapeDtypeStruct((B, S, POOL_WIDTH), F32),
                   jax.ShapeDtypeStruct((B, FOX_HEADS, LANES, S), BF16),
                   jax.ShapeDtypeStruct((B, S, FOX_HEADS * LANES), BF16),
                   jax.ShapeDtypeStruct((B, S // ta, FOX_HEADS * V_ROWS, ta), BF16),
                   jax.ShapeDtypeStruct((B, M, 2 * X_WIDTH), BF16)],
        scratch_shapes=[pltpu.VMEM((1, LANES), F32)],
        compiler_params=_params("arbitrary", "arbitrary"),
        name="in_proj",
    )(x, g, w_u, w_qkv, w_f, b_f, mem, g_mem, w_xkv)


def _fox_kernel(qt_ref, k_ref, vt_ref, o_ref, acc_ref):
    t = vt_ref.shape[3]
    n_tiles = vt_ref.shape[1]

    causal_t = (lax.broadcasted_iota(jnp.int32, (t, t), 0) <=
                lax.broadcasted_iota(jnp.int32, (t, t), 1))

    items = [(qi, j, hd) for qi in range(n_tiles) for j in range(qi + 1)
             for hd in range(FOX_HEADS)]

    def scores(idx):
        qi, j, hd = items[idx]
        return jnp.dot(k_ref[0, j * t:(j + 1) * t, hd * LANES:(hd + 1) * LANES],
                       qt_ref[0, hd, :, qi * t:(qi + 1) * t],
                       preferred_element_type=F32)

    pending = [scores(idx) for idx in range(SCORE_LOOKAHEAD)]
    ms = None
    for idx, (qi, j, hd) in enumerate(items):
        last = j == qi
        acc = acc_ref.at[qi % 2]
        if j == 0 and hd == 0:
            acc[...] = jnp.zeros_like(acc)
            ms = [jnp.full((1, t), NEG_BIG, F32) for _ in range(FOX_HEADS)]
        st = pending.pop(0)
        if idx + SCORE_LOOKAHEAD < len(items):
            pending.append(scores(idx + SCORE_LOOKAHEAD))
        if last:
            st = jnp.where(causal_t, st, NEG_BIG)
        m_new = jnp.maximum(ms[hd], jnp.max(st, axis=0, keepdims=True))
        alpha = jnp.exp2(ms[hd] - m_new)
        pt = jnp.exp2(st - m_new).astype(BF16)
        pv = jnp.dot(vt_ref[0, j, hd * V_ROWS:(hd + 1) * V_ROWS, :], pt,
                     preferred_element_type=F32)
        acc[hd] = alpha * acc[hd] + pv
        ms[hd] = m_new
        if last and hd == FOX_HEADS - 1:
            o_t = jnp.concatenate([acc[h, :FOX_DH, :] / acc[h, FOX_DH:FOX_DH + 1, :]
                                   for h in range(FOX_HEADS)], axis=0)
            o_ref[0, qi * t:(qi + 1) * t, :] = o_t.T.astype(BF16)


def _fox_attention(qat, ka, vt, t):
    B, S, _ = ka.shape
    return pl.pallas_call(
        _fox_kernel,
        grid=(B,),
        in_specs=[pl.BlockSpec((1, FOX_HEADS, LANES, S), lambda b: (b, 0, 0, 0)),
                  pl.BlockSpec((1, S, FOX_HEADS * LANES), lambda b: (b, 0, 0)),
                  pl.BlockSpec((1, S // t, FOX_HEADS * V_ROWS, t), lambda b: (b, 0, 0, 0))],
        out_specs=pl.BlockSpec((1, S, FOX_WIDTH), lambda b: (b, 0, 0)),
        out_shape=jax.ShapeDtypeStruct((B, S, FOX_WIDTH), BF16),
        scratch_shapes=[pltpu.VMEM((2, FOX_HEADS, V_ROWS, t), F32)],
        compiler_params=_params("arbitrary"),
        name="fox_attn",
    )(qat, ka, vt)


def _pool_fold_kernel(pw_ref, ps_ref, wpo_ref, out_ref):
    for g in range(len(POOL_WINDOWS)):
        rows = slice(g * POOL_GC, (g + 1) * POOL_GC)
        scaled = pw_ref[g] * ps_ref[:, rows]
        out_ref[rows, :] = jnp.dot(scaled, wpo_ref[rows, :], preferred_element_type=F32,
                                   precision=lax.Precision.HIGHEST).astype(BF16)


def _pool_fold(pool_w, pool_scale, w_po):
    return pl.pallas_call(
        _pool_fold_kernel,
        out_shape=jax.ShapeDtypeStruct(w_po.shape, BF16),
        name="pool_fold",
    )(pool_w, pool_scale, w_po)


def _post_kernel(x_ref, u_ref, o_ref, kv_ref, gmix_ref, wg_ref, bg_ref, wpool_ref,
                 wfo_ref, wo_ref, gx_ref, wxq_ref, wxo_ref, x2_ref, ubuf_ref):
    tm = x_ref.shape[1]
    D = x_ref.shape[2]
    si = pl.program_id(1)
    rows = tm // POST_STREAMS
    streams = [slice(r * rows, (r + 1) * rows) for r in range(POST_STREAMS)]
    nt_dims = (((1,), (1,)), ((), ()))
    dot = functools.partial(jnp.dot, preferred_element_type=F32)

    @pl.when(si == 0)
    def _():
        ubuf_ref[0:MAX_WINDOW, :] = jnp.zeros((MAX_WINDOW, POOL_WIDTH), F32)

    @pl.when(si > 0)
    def _():
        ubuf_ref[0:MAX_WINDOW, :] = ubuf_ref[tm:tm + MAX_WINDOW, :]

    ubuf_ref[MAX_WINDOW:, :] = u_ref[0]

    def pool_delta(r, g):
        w = POOL_WINDOWS[g]
        sl = slice(g * POOL_GC, (g + 1) * POOL_GC)
        ws = ubuf_ref[r.start:r.stop + MAX_WINDOW, sl]
        k = 1
        while k < w:
            ws = ws + pltpu.roll(ws, k, 0)
            k *= 2
        pos = si * tm + r.start + lax.broadcasted_iota(jnp.int32, (rows, 1), 0)
        cnt = jnp.minimum(pos + 1, w).astype(F32)
        return (ws[MAX_WINDOW:, :] / cnt - u_ref[0, r, sl]).astype(BF16)

    y_pools = [dot(jnp.concatenate([pool_delta(r, g) for g in range(len(POOL_WINDOWS))], axis=1),
                   wpool_ref[...]) for r in streams]

    xs = [x_ref[0, r, :] for r in streams]
    hs = [_rmsnorm(x, gmix_ref[...]).astype(BF16) for x in xs]
    gates = [_sigmoid(dot(h, wg_ref[...]) + bg_ref[...]) for h in hs]
    y_foxs = [dot(o_ref[0, r, :], wfo_ref[...]) for r in streams]
    mixes = [(g[:, :D] * yp + g[:, D:] * yf).astype(BF16)
             for g, yp, yf in zip(gates, y_pools, y_foxs)]
    x1s = [x + dot(m, wo_ref[...]) for x, m in zip(xs, mixes)]

    hxs = [_rmsnorm(x1, gx_ref[...]).astype(BF16) for x1 in x1s]
    qxs = [dot(hx, wxq_ref[...]).astype(BF16) for hx in hxs]
    scores = [[lax.dot_general(qx[:, hd * X_DH:(hd + 1) * X_DH],
                               kv_ref[0, :, hd * X_DH:(hd + 1) * X_DH], nt_dims,
                               preferred_element_type=F32) * (X_DH ** -0.5)
               for hd in range(X_HEADS)] for qx in qxs]
    oxs = []
    for sc in scores:
        heads = []
        for hd, s in enumerate(sc):
            p = jnp.exp(s - jnp.max(s, axis=1, keepdims=True))
            l = jnp.sum(p, axis=1, keepdims=True)
            vh = kv_ref[0, :, X_WIDTH + hd * X_DH:X_WIDTH + (hd + 1) * X_DH]
            heads.append(dot(p.astype(BF16), vh) / l)
        oxs.append(jnp.concatenate(heads, axis=1).astype(BF16))
    for r, x1, ox in zip(streams, x1s, oxs):
        x2_ref[0, r, :] = x1 + dot(ox, wxo_ref[...])


def _post(x, u, o, kv, gmix, w_g, b_g, w_pool, w_fo, w_o, gx, w_xq, w_xo, tm):
    B, S, D = x.shape
    M = kv.shape[1]
    row = lambda n: pl.BlockSpec((1, tm, n), lambda b, s: (b, s, 0))
    consts = [gmix, w_g, b_g, w_pool, w_fo, w_o, gx, w_xq, w_xo]
    return pl.pallas_call(
        _post_kernel,
        grid=(B, S // tm),
        in_specs=[row(D), row(POOL_WIDTH), row(FOX_WIDTH),
                  pl.BlockSpec((1, M, 2 * X_WIDTH), lambda b, s: (b, 0, 0))]
                 + [_const_spec(a.shape) for a in consts],
        out_specs=row(D),
        out_shape=jax.ShapeDtypeStruct((B, S, D), F32),
        scratch_shapes=[pltpu.VMEM((tm + MAX_WINDOW, POOL_WIDTH), F32)],
        compiler_params=_params("arbitrary", "arbitrary"),
        name="post",
    )(x, u, o, kv, *consts)


def _ffn_kernel(x_ref, g_ref, wi_ref, wo_ref, gfin_ref, out_ref, *, final_norm):
    d_ff = wo_ref.shape[0]
    rows = x_ref.shape[0] // FFN_STREAMS
    streams = [slice(r * rows, (r + 1) * rows) for r in range(FFN_STREAMS)]
    dot = functools.partial(jnp.dot, preferred_element_type=F32)
    xs = [x_ref[r, :] for r in streams]
    hfs = [_rmsnorm(x, g_ref[...]).astype(BF16) for x in xs]
    accs = [None] * FFN_STREAMS
    for c0 in range(0, d_ff, FFN_CHUNK):
        c1 = min(c0 + FFN_CHUNK, d_ff)
        gts = [dot(hf, wi_ref[:, c0:c1]) for hf in hfs]
        ups = [dot(hf, wi_ref[:, d_ff + c0:d_ff + c1]) for hf in hfs]
        for i, (gt, up) in enumerate(zip(gts, ups)):
            part = dot((gt * _sigmoid(gt) * up).astype(BF16), wo_ref[c0:c1, :])
            accs[i] = part if accs[i] is None else accs[i] + part
    for r, x, acc in zip(streams, xs, accs):
        x3 = x + acc
        out_ref[r, :] = _rmsnorm(x3, gfin_ref[...]) if final_norm else x3


def _ffn(x2, g, w_i, w_o, g_fin, tm, final_norm):
    T, D = x2.shape
    return pl.pallas_call(
        functools.partial(_ffn_kernel, final_norm=final_norm),
        grid=(T // tm,),
        in_specs=[pl.BlockSpec((tm, D), lambda i: (i, 0)), _const_spec((1, D)),
                  pl.BlockSpec(w_i.shape, lambda i: (0, 0), pipeline_mode=pl.Buffered(1)),
                  pl.BlockSpec(w_o.shape, lambda i: (0, 0), pipeline_mode=pl.Buffered(1)),
                  _const_spec((1, D))],
        out_specs=pl.BlockSpec((tm, D), lambda i: (i, 0)),
        out_shape=jax.ShapeDtypeStruct((T, D), F32),
        compiler_params=_params("arbitrary"),
        name="ffn",
    )(x2, g, w_i, w_o, g_fin)


def kernel(x, mem, norm_mix_g, w_in, b_forget, b_gate, pool_w, pool_scale, w_pool_out, w_fox_out,
           w_out, norm_x_g, norm_mem_g, w_xq, w_xkv, w_xo, norm_ffn_g, w_ffn_in, w_ffn_out,
           norm_final_g):
    B, S, D = x.shape
    depth = w_in.shape[0]
    qkv_end = POOL_WIDTH + 3 * FOX_WIDTH
    f_end = qkv_end + FOX_HEADS
    row2 = lambda a: a.reshape(1, -1)
    for l in range(depth):
        wl = w_in[l]
        rep = AUG_TERMS * FOX_HEADS
        w_f = jnp.pad(jnp.repeat(wl[:, qkv_end:f_end], AUG_TERMS, axis=1),
                      ((0, 0), (0, LANES - rep))).astype(BF16)
        b_f = jnp.pad(jnp.repeat(b_forget[l], AUG_TERMS), (0, LANES - rep)).reshape(1, LANES)
        u, qat, ka, vt, kv = _in_proj(x, row2(norm_mix_g[l]), wl[:, :POOL_WIDTH].astype(BF16),
                                     wl[:, POOL_WIDTH:qkv_end].astype(BF16), w_f, b_f,
                                     mem, row2(norm_mem_g[l]), w_xkv[l].astype(BF16),
                                     IN_TILE, ATT_TILE)
        o = _fox_attention(qat, ka, vt, ATT_TILE)
        x2 = _post(x, u, o, kv, row2(norm_mix_g[l]), wl[:, f_end:].astype(BF16), row2(b_gate[l]),
                   _pool_fold(pool_w[l], row2(pool_scale[l]), w_pool_out[l]),
                   w_fox_out[l].astype(BF16), w_out[l].astype(BF16), row2(norm_x_g[l]),
                   w_xq[l].astype(BF16), w_xo[l].astype(BF16), POST_TILE)
        x = _ffn(x2.reshape(B * S, D), row2(norm_ffn_g[l]), w_ffn_in[l].astype(BF16),
                 w_ffn_out[l].astype(BF16), row2(norm_final_g), FFN_TILE,
                 final_norm=(l == depth - 1)).reshape(B, S, D)
    return x
```

```python
import functools

import jax
import jax.numpy as jnp
from jax import lax
from jax.experimental import pallas as pl
from jax.experimental.pallas import tpu as pltpu

F32 = jnp.float32
BF16 = jnp.bfloat16

EPS = 1e-6
POOL_WINDOWS = (2, 4, 8, 16)
POOL_GC = 128
POOL_WIDTH = 512
MAX_WINDOW = 16
FOX_HEADS = 8
FOX_DH = 64
FOX_WIDTH = 512
X_HEADS = 4
X_DH = 128
X_WIDTH = 512
LANES = 128
MXU_COLS = 256
VMEM_LIMIT = 56 * 1024 * 1024
NEG_BIG = -1e30
AUG_TERMS = 3
LOG2_E = 1.4426950408889634
V_ROWS = FOX_DH + 16

ATT_TILE = 256
SCORE_LOOKAHEAD = 6
IN_TILE = 1024
IN_STREAMS = 2
POST_TILE = 1024
POST_STREAMS = 2
FFN_TILE = 1024
FFN_STREAMS = 2
FFN_CHUNK = 1024
WEIGHT_ROWS = 256


def _rmsnorm(x, g):
    return x * lax.rsqrt(jnp.mean(x * x, axis=-1, keepdims=True) + EPS) * g


def _log_sigmoid(x):
    return jnp.minimum(x, 0.0) - jnp.log1p(jnp.exp(-jnp.abs(x)))


def _sigmoid(x):
    return 1.0 / (1.0 + jnp.exp(-x))


def _const_spec(shape):
    return pl.BlockSpec(shape, lambda *_: (0,) * len(shape), pipeline_mode=pl.Buffered(1))


def _params(*sem):
    return pltpu.CompilerParams(dimension_semantics=sem, vmem_limit_bytes=VMEM_LIMIT)


def _split_cast_kernel(w_ref, wu_ref, wqkv_ref, wg_ref):
    qkv_end = POOL_WIDTH + 3 * FOX_WIDTH
    w = w_ref[...]
    wu_ref[...] = w[:, :POOL_WIDTH].astype(BF16)
    wqkv_ref[...] = w[:, POOL_WIDTH:qkv_end].astype(BF16)
    wg_ref[...] = w[:, qkv_end + FOX_HEADS:].astype(BF16)


def _split_cast(w, rows):
    D, cols = w.shape
    qkv_end = POOL_WIDTH + 3 * FOX_WIDTH
    n_gate = cols - qkv_end - FOX_HEADS
    blk = lambda n: pl.BlockSpec((rows, n), lambda i: (i, 0))
    return pl.pallas_call(
        _split_cast_kernel,
        grid=(D // rows,),
        in_specs=[blk(cols)],
        out_specs=[blk(POOL_WIDTH), blk(3 * FOX_WIDTH), blk(n_gate)],
        out_shape=[jax.ShapeDtypeStruct((D, POOL_WIDTH), BF16),
                   jax.ShapeDtypeStruct((D, 3 * FOX_WIDTH), BF16),
                   jax.ShapeDtypeStruct((D, n_gate), BF16)],
        compiler_params=_params("arbitrary"),
        name="split_cast",
    )(w)


def _split3(c):
    hi = c.astype(BF16).astype(F32)
    r = c - hi
    lo = r.astype(BF16).astype(F32)
    return hi, lo, r - lo


def _in_proj_kernel(x_ref, g_ref, wu_ref, wqkv_ref, wf_ref, bf_ref, mem_ref, gmem_ref, wxkv_ref,
                    u_ref, qat_ref, ka_ref, vt_ref, kv_ref, csum_ref):
    tm = x_ref.shape[1]
    rows = tm // IN_STREAMS
    streams = [slice(r * rows, (r + 1) * rows) for r in range(IN_STREAMS)]
    dot = functools.partial(jnp.dot, preferred_element_type=F32)

    @pl.when(pl.program_id(1) == 0)
    def _():
        csum_ref[...] = jnp.zeros_like(csum_ref)
        mem_n = _rmsnorm(mem_ref[0], gmem_ref[...]).astype(BF16)
        kv_ref[0] = dot(mem_n, wxkv_ref[...]).astype(BF16)

    hs = [_rmsnorm(x_ref[0, r, :], g_ref[...]).astype(BF16) for r in streams]

    logfs = [_log_sigmoid(dot(h, wf_ref[...]) + bf_ref[...]) for h in hs]

    ta = vt_ref.shape[3]
    ones_rows = jnp.where(lax.broadcasted_iota(jnp.int32, (V_ROWS - FOX_DH, ta), 0) == 0,
                          1.0, 0.0).astype(BF16)
    for i, h in enumerate(hs):
        v = dot(h, wqkv_ref[:, 2 * FOX_WIDTH:])
        for sub in range(rows // ta):
            v_t = v[sub * ta:(sub + 1) * ta, :].T.astype(BF16)
            tile = i * (rows // ta) + sub
            for hd in range(FOX_HEADS):
                vt_ref[0, tile, hd * V_ROWS:hd * V_ROWS + FOX_DH, :] = v_t[hd * FOX_DH:(hd + 1) * FOX_DH, :]
                vt_ref[0, tile, hd * V_ROWS + FOX_DH:(hd + 1) * V_ROWS, :] = ones_rows

    row = lax.broadcasted_iota(jnp.int32, (rows, rows), 0)
    col = lax.broadcasted_iota(jnp.int32, (rows, rows), 1)
    tri = jnp.where(row >= col, 1.0, 0.0).astype(BF16)
    parts = [dot(tri, jnp.concatenate(_split3(logf), axis=1).astype(BF16)) for logf in logfs]
    lane = lax.broadcasted_iota(jnp.int32, (1, LANES), 1)
    piece = lane % AUG_TERMS
    c3s = []
    carry = csum_ref[...]
    for p in parts:
        c = p[:, :LANES] + p[:, LANES:2 * LANES] + p[:, 2 * LANES:] + carry
        carry = c[rows - 1:rows, :]
        c_hi, c_lo, c_lo2 = _split3(c * LOG2_E)
        c3s.append(jnp.where(piece == 0, c_hi, jnp.where(piece == 1, c_lo, c_lo2)))
    csum_ref[...] = carry

    heads_per_chunk = MXU_COLS // FOX_DH
    for chunk in range(FOX_WIDTH // MXU_COLS):
        cols = slice(chunk * MXU_COLS, (chunk + 1) * MXU_COLS)
        kcols = slice(FOX_WIDTH + chunk * MXU_COLS, FOX_WIDTH + (chunk + 1) * MXU_COLS)
        qcs = [dot(h, wqkv_ref[:, cols]) * (FOX_DH ** -0.5 * LOG2_E) for h in hs]
        kcs = [dot(h, wqkv_ref[:, kcols]) for h in hs]
        for r, c3, qc, kc in zip(streams, c3s, qcs, kcs):
            c3_neg = -c3
            for sub_head in range(heads_per_chunk):
                head = chunk * heads_per_chunk + sub_head
                pr, hh = divmod(sub_head, 2)
                base = FOX_DH if hh == 0 else 0
                a = lane - base
                in_aug = (a >= 0) & (a < FOX_DH)
                rq = pltpu.roll(c3, (base - AUG_TERMS * head) % LANES, 1)
                rk = pltpu.roll(c3_neg, (base + AUG_TERMS - AUG_TERMS * head) % LANES, 1)
                q_fill = jnp.where((a >= AUG_TERMS) & (a < 2 * AUG_TERMS), 1.0, 0.0)
                k_fill = jnp.where((a >= 0) & (a < AUG_TERMS), 1.0, 0.0)
                qp = qc[:, pr * LANES:(pr + 1) * LANES]
                kp = kc[:, pr * LANES:(pr + 1) * LANES]
                hs_ = slice(head * LANES, (head + 1) * LANES)
                qat_ref[0, head, :, r] = jnp.where((a >= 0) & (a < AUG_TERMS), rq,
                                                   jnp.where(in_aug, q_fill, qp)).T.astype(BF16)
                ka_ref[0, r, hs_] = jnp.where((a >= AUG_TERMS) & (a < 2 * AUG_TERMS), rk,
                                              jnp.where(in_aug, k_fill, kp)).astype(BF16)

    for r, h in zip(streams, hs):
        u_ref[0, r, :] = dot(h, wu_ref[...])


def _in_proj(x, g, w_u, w_qkv, w_f, b_f, mem, g_mem, w_xkv, tm, ta):
    B, S, D = x.shape
    M = mem.shape[1]
    row = lambda n: pl.BlockSpec((1, tm, n), lambda b, s: (b, s, 0))
    per_seq = lambda n: pl.BlockSpec((1, M, n), lambda b, s: (b, 0, 0))
    return pl.pallas_call(
        _in_proj_kernel,
        grid=(B, S // tm),
        in_specs=[row(D), _const_spec((1, D)), _const_spec(w_u.shape), _const_spec(w_qkv.shape),
                  _const_spec(w_f.shape), _const_spec((1, LANES)),
                  per_seq(D), _const_spec((1, D)), _const_spec(w_xkv.shape)],
        out_specs=[row(POOL_WIDTH),
                   pl.BlockSpec((1, FOX_HEADS, LANES, tm), lambda b, s: (b, 0, 0, s)),
                   row(FOX_HEADS * LANES),
                   pl.BlockSpec((1, tm // ta, FOX_HEADS * V_ROWS, ta), lambda b, s: (b, s, 0, 0)),
                   per_seq(2 * X_WIDTH)],
        out_shape=[jax.ShapeDtypeStruct((B, S, POOL_WIDTH), F32),
                   jax.ShapeDtypeStruct((B, FOX_HEADS, LANES, S), BF16),
                   jax.ShapeDtypeStruct((B, S, FOX_HEADS * LANES), BF16),
                   jax.ShapeDtypeStruct((B, S // ta, FOX_HEADS * V_ROWS, ta), BF16),
                   jax.ShapeDtypeStruct((B, M, 2 * X_WIDTH), BF16)],
        scratch_shapes=[pltpu.VMEM((1, LANES), F32)],
        compiler_params=_params("arbitrary", "arbitrary"),
        name="in_proj",
    )(x, g, w_u, w_qkv, w_f, b_f, mem, g_mem, w_xkv)


def _fox_kernel(qt_ref, k_ref, vt_ref, o_ref, acc_ref):
    t = vt_ref.shape[3]
    n_tiles = vt_ref.shape[1]

    causal_t = (lax.broadcasted_iota(jnp.int32, (t, t), 0) <=
                lax.broadcasted_iota(jnp.int32, (t, t), 1))

    items = [(qi, j, hd) for qi in range(n_tiles) for j in range(qi + 1)
             for hd in range(FOX_HEADS)]

    def scores(idx):
        qi, j, hd = items[idx]
        return jnp.dot(k_ref[0, j * t:(j + 1) * t, hd * LANES:(hd + 1) * LANES],
                       qt_ref[0, hd, :, qi * t:(qi + 1) * t],
                       preferred_element_type=F32)

    pending = [scores(idx) for idx in range(SCORE_LOOKAHEAD)]
    ms = None
    for idx, (qi, j, hd) in enumerate(items):
        last = j == qi
        acc = acc_ref.at[qi % 2]
        if j == 0 and hd == 0:
            acc[...] = jnp.zeros_like(acc)
            ms = [jnp.full((1, t), NEG_BIG, F32) for _ in range(FOX_HEADS)]
        st = pending.pop(0)
        if idx + SCORE_LOOKAHEAD < len(items):
            pending.append(scores(idx + SCORE_LOOKAHEAD))
        if last:
            st = jnp.where(causal_t, st, NEG_BIG)
        m_new = jnp.maximum(ms[hd], jnp.max(st, axis=0, keepdims=True))
        alpha = jnp.exp2(ms[hd] - m_new)
        pt = jnp.exp2(st - m_new).astype(BF16)
        pv = jnp.dot(vt_ref[0, j, hd * V_ROWS:(hd + 1) * V_ROWS, :], pt,
                     preferred_element_type=F32)
        acc[hd] = alpha * acc[hd] + pv
        ms[hd] = m_new
        if last and hd == FOX_HEADS - 1:
            o_t = jnp.concatenate([acc[h, :FOX_DH, :] / acc[h, FOX_DH:FOX_DH + 1, :]
                                   for h in range(FOX_HEADS)], axis=0)
            o_ref[0, qi * t:(qi + 1) * t, :] = o_t.T.astype(BF16)


def _fox_attention(qat, ka, vt, t):
    B, S, _ = ka.shape
    return pl.pallas_call(
        _fox_kernel,
        grid=(B,),
        in_specs=[pl.BlockSpec((1, FOX_HEADS, LANES, S), lambda b: (b, 0, 0, 0)),
                  pl.BlockSpec((1, S, FOX_HEADS * LANES), lambda b: (b, 0, 0)),
                  pl.BlockSpec((1, S // t, FOX_HEADS * V_ROWS, t), lambda b: (b, 0, 0, 0))],
        out_specs=pl.BlockSpec((1, S, FOX_WIDTH), lambda b: (b, 0, 0)),
        out_shape=jax.ShapeDtypeStruct((B, S, FOX_WIDTH), BF16),
        scratch_shapes=[pltpu.VMEM((2, FOX_HEADS, V_ROWS, t), F32)],
        compiler_params=_params("arbitrary"),
        name="fox_attn",
    )(qat, ka, vt)


def _pool_fold_kernel(pw_ref, ps_ref, wpo_ref, out_ref):
    for g in range(len(POOL_WINDOWS)):
        rows = slice(g * POOL_GC, (g + 1) * POOL_GC)
        scaled = pw_ref[g] * ps_ref[:, rows]
        out_ref[rows, :] = jnp.dot(scaled, wpo_ref[rows, :], preferred_element_type=F32,
                                   precision=lax.Precision.HIGHEST).astype(BF16)


def _pool_fold(pool_w, pool_scale, w_po):
    return pl.pallas_call(
        _pool_fold_kernel,
        out_shape=jax.ShapeDtypeStruct(w_po.shape, BF16),
        name="pool_fold",
    )(pool_w, pool_scale, w_po)


def _post_kernel(x_ref, u_ref, o_ref, kv_ref, gmix_ref, wg_ref, bg_ref, wpool_ref,
                 wfo_ref, wo_ref, gx_ref, wxq_ref, wxo_ref, x2_ref, ubuf_ref):
    tm = x_ref.shape[1]
    D = x_ref.shape[2]
    si = pl.program_id(1)
    rows = tm // POST_STREAMS
    streams = [slice(r * rows, (r + 1) * rows) for r in range(POST_STREAMS)]
    nt_dims = (((1,), (1,)), ((), ()))
    dot = functools.partial(jnp.dot, preferred_element_type=F32)

    @pl.when(si == 0)
    def _():
        ubuf_ref[0:MAX_WINDOW, :] = jnp.zeros((MAX_WINDOW, POOL_WIDTH), F32)

    @pl.when(si > 0)
    def _():
        ubuf_ref[0:MAX_WINDOW, :] = ubuf_ref[tm:tm + MAX_WINDOW, :]

    ubuf_ref[MAX_WINDOW:, :] = u_ref[0]

    def pool_delta(r, g):
        w = POOL_WINDOWS[g]
        sl = slice(g * POOL_GC, (g + 1) * POOL_GC)
        ws = ubuf_ref[r.start:r.stop + MAX_WINDOW, sl]
        k = 1
        while k < w:
            ws = ws + pltpu.roll(ws, k, 0)
            k *= 2
        pos = si * tm + r.start + lax.broadcasted_iota(jnp.int32, (rows, 1), 0)
        cnt = jnp.minimum(pos + 1, w).astype(F32)
        return (ws[MAX_WINDOW:, :] / cnt - u_ref[0, r, sl]).astype(BF16)

    y_pools = [dot(jnp.concatenate([pool_delta(r, g) for g in range(len(POOL_WINDOWS))], axis=1),
                   wpool_ref[...]) for r in streams]

    xs = [x_ref[0, r, :] for r in streams]
    hs = [_rmsnorm(x, gmix_ref[...]).astype(BF16) for x in xs]
    gates = [_sigmoid(dot(h, wg_ref[...]) + bg_ref[...]) for h in hs]
    y_foxs = [dot(o_ref[0, r, :], wfo_ref[...]) for r in streams]
    mixes = [(g[:, :D] * yp + g[:, D:] * yf).astype(BF16)
             for g, yp, yf in zip(gates, y_pools, y_foxs)]
    x1s = [x + dot(m, wo_ref[...]) for x, m in zip(xs, mixes)]

    hxs = [_rmsnorm(x1, gx_ref[...]).astype(BF16) for x1 in x1s]
    qxs = [dot(hx, wxq_ref[...]).astype(BF16) for hx in hxs]
    scores = [[lax.dot_general(qx[:, hd * X_DH:(hd + 1) * X_DH],
                               kv_ref[0, :, hd * X_DH:(hd + 1) * X_DH], nt_dims,
                               preferred_element_type=F32) * (X_DH ** -0.5)
               for hd in range(X_HEADS)] for qx in qxs]
    oxs = []
    for sc in scores:
        heads = []
        for hd, s in enumerate(sc):
            p = jnp.exp(s - jnp.max(s, axis=1, keepdims=True))
            l = jnp.sum(p, axis=1, keepdims=True)
            vh = kv_ref[0, :, X_WIDTH + hd * X_DH:X_WIDTH + (hd + 1) * X_DH]
            heads.append(dot(p.astype(BF16), vh) / l)
        oxs.append(jnp.concatenate(heads, axis=1).astype(BF16))
    for r, x1, ox in zip(streams, x1s, oxs):
        x2_ref[0, r, :] = x1 + dot(ox, wxo_ref[...])


def _post(x, u, o, kv, gmix, w_g, b_g, w_pool, w_fo, w_o, gx, w_xq, w_xo, tm):
    B, S, D = x.shape
    M = kv.shape[1]
    row = lambda n: pl.BlockSpec((1, tm, n), lambda b, s: (b, s, 0))
    consts = [gmix, w_g, b_g, w_pool, w_fo, w_o, gx, w_xq, w_xo]
    return pl.pallas_call(
        _post_kernel,
        grid=(B, S // tm),
        in_specs=[row(D), row(POOL_WIDTH), row(FOX_WIDTH),
                  pl.BlockSpec((1, M, 2 * X_WIDTH), lambda b, s: (b, 0, 0))]
                 + [_const_spec(a.shape) for a in consts],
        out_specs=row(D),
        out_shape=jax.ShapeDtypeStruct((B, S, D), F32),
        scratch_shapes=[pltpu.VMEM((tm + MAX_WINDOW, POOL_WIDTH), F32)],
        compiler_params=_params("arbitrary", "arbitrary"),
        name="post",
    )(x, u, o, kv, *consts)


def _ffn_kernel(x_ref, g_ref, wi_ref, wo_ref, gfin_ref, out_ref, *, final_norm):
    d_ff = wo_ref.shape[0]
    rows = x_ref.shape[0] // FFN_STREAMS
    streams = [slice(r * rows, (r + 1) * rows) for r in range(FFN_STREAMS)]
    dot = functools.partial(jnp.dot, preferred_element_type=F32)
    xs = [x_ref[r, :] for r in streams]
    hfs = [_rmsnorm(x, g_ref[...]).astype(BF16) for x in xs]
    accs = [None] * FFN_STREAMS
    for c0 in range(0, d_ff, FFN_CHUNK):
        c1 = min(c0 + FFN_CHUNK, d_ff)
        gts = [dot(hf, wi_ref[:, c0:c1]) for hf in hfs]
        ups = [dot(hf, wi_ref[:, d_ff + c0:d_ff + c1]) for hf in hfs]
        for i, (gt, up) in enumerate(zip(gts, ups)):
            part = dot((gt * _sigmoid(gt) * up).astype(BF16), wo_ref[c0:c1, :])
            accs[i] = part if accs[i] is None else accs[i] + part
    for r, x, acc in zip(streams, xs, accs):
        x3 = x + acc
        out_ref[r, :] = _rmsnorm(x3, gfin_ref[...]) if final_norm else x3


def _ffn(x2, g, w_i, w_o, g_fin, tm, final_norm):
    T, D = x2.shape
    return pl.pallas_call(
        functools.partial(_ffn_kernel, final_norm=final_norm),
        grid=(T // tm,),
        in_specs=[pl.BlockSpec((tm, D), lambda i: (i, 0)), _const_spec((1, D)),
                  pl.BlockSpec(w_i.shape, lambda i: (0, 0), pipeline_mode=pl.Buffered(1)),
                  pl.BlockSpec(w_o.shape, lambda i: (0, 0), pipeline_mode=pl.Buffered(1)),
                  _const_spec((1, D))],
        out_specs=pl.BlockSpec((tm, D), lambda i: (i, 0)),
        out_shape=jax.ShapeDtypeStruct((T, D), F32),
        compiler_params=_params("arbitrary"),
        name="ffn",
    )(x2, g, w_i, w_o, g_fin)


def kernel(x, mem, norm_mix_g, w_in, b_forget, b_gate, pool_w, pool_scale, w_pool_out, w_fox_out,
           w_out, norm_x_g, norm_mem_g, w_xq, w_xkv, w_xo, norm_ffn_g, w_ffn_in, w_ffn_out,
           norm_final_g):
    B, S, D = x.shape
    depth = w_in.shape[0]
    qkv_end = POOL_WIDTH + 3 * FOX_WIDTH
    f_end = qkv_end + FOX_HEADS
    row2 = lambda a: a.reshape(1, -1)
    for l in range(depth):
        wl = w_in[l]
        w_u, w_qkv, w_g = _split_cast(wl, WEIGHT_ROWS)
        rep = AUG_TERMS * FOX_HEADS
        w_f = jnp.pad(jnp.repeat(wl[:, qkv_end:f_end], AUG_TERMS, axis=1),
                      ((0, 0), (0, LANES - rep))).astype(BF16)
        b_f = jnp.pad(jnp.repeat(b_forget[l], AUG_TERMS), (0, LANES - rep)).reshape(1, LANES)
        u, qat, ka, vt, kv = _in_proj(x, row2(norm_mix_g[l]), w_u, w_qkv, w_f, b_f,
                                     mem, row2(norm_mem_g[l]), w_xkv[l].astype(BF16),
                                     IN_TILE, ATT_TILE)
        o = _fox_attention(qat, ka, vt, ATT_TILE)
        x2 = _post(x, u, o, kv, row2(norm_mix_g[l]), w_g, row2(b_gate[l]),
                   _pool_fold(pool_w[l], row2(pool_scale[l]), w_pool_out[l]),
                   w_fox_out[l].astype(BF16), w_out[l].astype(BF16), row2(norm_x_g[l]),
                   w_xq[l].astype(BF16), w_xo[l].astype(BF16), POST_TILE)
        x = _ffn(x2.reshape(B * S, D), row2(norm_ffn_g[l]), w_ffn_in[l].astype(BF16),
                 w_ffn_out[l].astype(BF16), row2(norm_final_g), FFN_TILE,
                 final_norm=(l == depth - 1)).reshape(B, S, D)
    return x
```

```python
import functools

import jax
import jax.numpy as jnp
from jax import lax
from jax.experimental import pallas as pl
from jax.experimental.pallas import tpu as pltpu

F32 = jnp.float32
BF16 = jnp.bfloat16

EPS = 1e-6
POOL_WINDOWS = (2, 4, 8, 16)
POOL_GC = 128
POOL_WIDTH = 512
MAX_WINDOW = 16
FOX_HEADS = 8
FOX_DH = 64
FOX_WIDTH = 512
X_HEADS = 4
X_DH = 128
X_WIDTH = 512
LANES = 128
MXU_COLS = 256
VMEM_LIMIT = 56 * 1024 * 1024
NEG_BIG = -1e30
AUG_TERMS = 3
LOG2_E = 1.4426950408889634
V_ROWS = FOX_DH + 16

ATT_TILE = 256
SCORE_LOOKAHEAD = 6
IN_TILE = 1024
IN_STREAMS = 2
POST_TILE = 1024
POST_STREAMS = 2
FFN_TILE = 1024
FFN_STREAMS = 2
FFN_CHUNK = 1024
WEIGHT_ROWS = 256


def _rmsnorm(x, g):
    return x * lax.rsqrt(jnp.mean(x * x, axis=-1, keepdims=True) + EPS) * g


def _log_sigmoid(x):
    return jnp.minimum(x, 0.0) - jnp.log1p(jnp.exp(-jnp.abs(x)))


def _sigmoid(x):
    return 1.0 / (1.0 + jnp.exp(-x))


def _const_spec(shape):
    return pl.BlockSpec(shape, lambda *_: (0,) * len(shape), pipeline_mode=pl.Buffered(1))


def _params(*sem):
    return pltpu.CompilerParams(dimension_semantics=sem, vmem_limit_bytes=VMEM_LIMIT)


def _split_cast_kernel(wt_ref, wu_ref, wqkv_ref, wg_ref):
    qkv_end = POOL_WIDTH + 3 * FOX_WIDTH
    wu_ref[...] = wt_ref[:POOL_WIDTH, :].T.astype(BF16)
    wqkv_ref[...] = wt_ref[POOL_WIDTH:qkv_end, :].T.astype(BF16)
    wg_ref[...] = wt_ref[qkv_end + FOX_HEADS:, :].T.astype(BF16)


def _split_cast(wt, rows):
    cols, D = wt.shape
    qkv_end = POOL_WIDTH + 3 * FOX_WIDTH
    n_gate = cols - qkv_end - FOX_HEADS
    blk = lambda n: pl.BlockSpec((rows, n), lambda i: (i, 0))
    return pl.pallas_call(
        _split_cast_kernel,
        grid=(D // rows,),
        in_specs=[pl.BlockSpec((cols, rows), lambda i: (0, i))],
        out_specs=[blk(POOL_WIDTH), blk(3 * FOX_WIDTH), blk(n_gate)],
        out_shape=[jax.ShapeDtypeStruct((D, POOL_WIDTH), BF16),
                   jax.ShapeDtypeStruct((D, 3 * FOX_WIDTH), BF16),
                   jax.ShapeDtypeStruct((D, n_gate), BF16)],
        compiler_params=_params("arbitrary"),
        name="split_cast",
    )(wt)


def _split3(c):
    hi = c.astype(BF16).astype(F32)
    r = c - hi
    lo = r.astype(BF16).astype(F32)
    return hi, lo, r - lo


def _in_proj_kernel(x_ref, g_ref, wu_ref, wqkv_ref, wf_ref, bf_ref, mem_ref, gmem_ref, wxkv_ref,
                    u_ref, qat_ref, ka_ref, vt_ref, kv_ref, csum_ref):
    tm = x_ref.shape[1]
    rows = tm // IN_STREAMS
    streams = [slice(r * rows, (r + 1) * rows) for r in range(IN_STREAMS)]
    dot = functools.partial(jnp.dot, preferred_element_type=F32)

    @pl.when(pl.program_id(1) == 0)
    def _():
        csum_ref[...] = jnp.zeros_like(csum_ref)
        mem_n = _rmsnorm(mem_ref[0], gmem_ref[...]).astype(BF16)
        kv_ref[0] = dot(mem_n, wxkv_ref[...]).astype(BF16)

    hs = [_rmsnorm(x_ref[0, r, :], g_ref[...]).astype(BF16) for r in streams]

    logfs = [_log_sigmoid(dot(h, wf_ref[...]) + bf_ref[...]) for h in hs]

    ta = vt_ref.shape[3]
    ones_rows = jnp.where(lax.broadcasted_iota(jnp.int32, (V_ROWS - FOX_DH, ta), 0) == 0,
                          1.0, 0.0).astype(BF16)
    for i, h in enumerate(hs):
        v = dot(h, wqkv_ref[:, 2 * FOX_WIDTH:])
        for sub in range(rows // ta):
            v_t = v[sub * ta:(sub + 1) * ta, :].T.astype(BF16)
            tile = i * (rows // ta) + sub
            for hd in range(FOX_HEADS):
                vt_ref[0, tile, hd * V_ROWS:hd * V_ROWS + FOX_DH, :] = v_t[hd * FOX_DH:(hd + 1) * FOX_DH, :]
                vt_ref[0, tile, hd * V_ROWS + FOX_DH:(hd + 1) * V_ROWS, :] = ones_rows

    row = lax.broadcasted_iota(jnp.int32, (rows, rows), 0)
    col = lax.broadcasted_iota(jnp.int32, (rows, rows), 1)
    tri = jnp.where(row >= col, 1.0, 0.0).astype(BF16)
    parts = [dot(tri, jnp.concatenate(_split3(logf), axis=1).astype(BF16)) for logf in logfs]
    lane = lax.broadcasted_iota(jnp.int32, (1, LANES), 1)
    piece = lane % AUG_TERMS
    c3s = []
    carry = csum_ref[...]
    for p in parts:
        c = p[:, :LANES] + p[:, LANES:2 * LANES] + p[:, 2 * LANES:] + carry
        carry = c[rows - 1:rows, :]
        c_hi, c_lo, c_lo2 = _split3(c * LOG2_E)
        c3s.append(jnp.where(piece == 0, c_hi, jnp.where(piece == 1, c_lo, c_lo2)))
    csum_ref[...] = carry

    heads_per_chunk = MXU_COLS // FOX_DH
    for chunk in range(FOX_WIDTH // MXU_COLS):
        cols = slice(chunk * MXU_COLS, (chunk + 1) * MXU_COLS)
        kcols = slice(FOX_WIDTH + chunk * MXU_COLS, FOX_WIDTH + (chunk + 1) * MXU_COLS)
        qcs = [dot(h, wqkv_ref[:, cols]) * (FOX_DH ** -0.5 * LOG2_E) for h in hs]
        kcs = [dot(h, wqkv_ref[:, kcols]) for h in hs]
        for r, c3, qc, kc in zip(streams, c3s, qcs, kcs):
            c3_neg = -c3
            for sub_head in range(heads_per_chunk):
                head = chunk * heads_per_chunk + sub_head
                pr, hh = divmod(sub_head, 2)
                base = FOX_DH if hh == 0 else 0
                a = lane - base
                in_aug = (a >= 0) & (a < FOX_DH)
                rq = pltpu.roll(c3, (base - AUG_TERMS * head) % LANES, 1)
                rk = pltpu.roll(c3_neg, (base + AUG_TERMS - AUG_TERMS * head) % LANES, 1)
                q_fill = jnp.where((a >= AUG_TERMS) & (a < 2 * AUG_TERMS), 1.0, 0.0)
                k_fill = jnp.where((a >= 0) & (a < AUG_TERMS), 1.0, 0.0)
                qp = qc[:, pr * LANES:(pr + 1) * LANES]
                kp = kc[:, pr * LANES:(pr + 1) * LANES]
                hs_ = slice(head * LANES, (head + 1) * LANES)
                qat_ref[0, head, :, r] = jnp.where((a >= 0) & (a < AUG_TERMS), rq,
                                                   jnp.where(in_aug, q_fill, qp)).T.astype(BF16)
                ka_ref[0, r, hs_] = jnp.where((a >= AUG_TERMS) & (a < 2 * AUG_TERMS), rk,
                                              jnp.where(in_aug, k_fill, kp)).astype(BF16)

    for r, h in zip(streams, hs):
        u_ref[0, r, :] = dot(h, wu_ref[...])


def _in_proj(x, g, w_u, w_qkv, w_f, b_f, mem, g_mem, w_xkv, tm, ta):
    B, S, D = x.shape
    M = mem.shape[1]
    row = lambda n: pl.BlockSpec((1, tm, n), lambda b, s: (b, s, 0))
    per_seq = lambda n: pl.BlockSpec((1, M, n), lambda b, s: (b, 0, 0))
    return pl.pallas_call(
        _in_proj_kernel,
        grid=(B, S // tm),
        in_specs=[row(D), _const_spec((1, D)), _const_spec(w_u.shape), _const_spec(w_qkv.shape),
                  _const_spec(w_f.shape), _const_spec((1, LANES)),
                  per_seq(D), _const_spec((1, D)), _const_spec(w_xkv.shape)],
        out_specs=[row(POOL_WIDTH),
                   pl.BlockSpec((1, FOX_HEADS, LANES, tm), lambda b, s: (b, 0, 0, s)),
                   row(FOX_HEADS * LANES),
                   pl.BlockSpec((1, tm // ta, FOX_HEADS * V_ROWS, ta), lambda b, s: (b, s, 0, 0)),
                   per_seq(2 * X_WIDTH)],
        out_shape=[jax.ShapeDtypeStruct((B, S, POOL_WIDTH), F32),
                   jax.ShapeDtypeStruct((B, FOX_HEADS, LANES, S), BF16),
                   jax.ShapeDtypeStruct((B, S, FOX_HEADS * LANES), BF16),
                   jax.ShapeDtypeStruct((B, S // ta, FOX_HEADS * V_ROWS, ta), BF16),
                   jax.ShapeDtypeStruct((B, M, 2 * X_WIDTH), BF16)],
        scratch_shapes=[pltpu.VMEM((1, LANES), F32)],
        compiler_params=_params("arbitrary", "arbitrary"),
        name="in_proj",
    )(x, g, w_u, w_qkv, w_f, b_f, mem, g_mem, w_xkv)


def _fox_kernel(qt_ref, k_ref, vt_ref, o_ref, acc_ref):
    t = vt_ref.shape[3]
    n_tiles = vt_ref.shape[1]

    causal_t = (lax.broadcasted_iota(jnp.int32, (t, t), 0) <=
                lax.broadcasted_iota(jnp.int32, (t, t), 1))

    items = [(qi, j, hd) for qi in range(n_tiles) for j in range(qi + 1)
             for hd in range(FOX_HEADS)]

    def scores(idx):
        qi, j, hd = items[idx]
        return jnp.dot(k_ref[0, j * t:(j + 1) * t, hd * LANES:(hd + 1) * LANES],
                       qt_ref[0, hd, :, qi * t:(qi + 1) * t],
                       preferred_element_type=F32)

    pending = [scores(idx) for idx in range(SCORE_LOOKAHEAD)]
    ms = None
    for idx, (qi, j, hd) in enumerate(items):
        last = j == qi
        acc = acc_ref.at[qi % 2]
        if j == 0 and hd == 0:
            acc[...] = jnp.zeros_like(acc)
            ms = [jnp.full((1, t), NEG_BIG, F32) for _ in range(FOX_HEADS)]
        st = pending.pop(0)
        if idx + SCORE_LOOKAHEAD < len(items):
            pending.append(scores(idx + SCORE_LOOKAHEAD))
        if last:
            st = jnp.where(causal_t, st, NEG_BIG)
        m_new = jnp.maximum(ms[hd], jnp.max(st, axis=0, keepdims=True))
        alpha = jnp.exp2(ms[hd] - m_new)
        pt = jnp.exp2(st - m_new).astype(BF16)
        pv = jnp.dot(vt_ref[0, j, hd * V_ROWS:(hd + 1) * V_ROWS, :], pt,
                     preferred_element_type=F32)
        acc[hd] = alpha * acc[hd] + pv
        ms[hd] = m_new
        if last and hd == FOX_HEADS - 1:
            o_t = jnp.concatenate([acc[h, :FOX_DH, :] / acc[h, FOX_DH:FOX_DH + 1, :]
                                   for h in range(FOX_HEADS)], axis=0)
            o_ref[0, qi * t:(qi + 1) * t, :] = o_t.T.astype(BF16)


def _fox_attention(qat, ka, vt, t):
    B, S, _ = ka.shape
    return pl.pallas_call(
        _fox_kernel,
        grid=(B,),
        in_specs=[pl.BlockSpec((1, FOX_HEADS, LANES, S), lambda b: (b, 0, 0, 0)),
                  pl.BlockSpec((1, S, FOX_HEADS * LANES), lambda b: (b, 0, 0)),
                  pl.BlockSpec((1, S // t, FOX_HEADS * V_ROWS, t), lambda b: (b, 0, 0, 0))],
        out_specs=pl.BlockSpec((1, S, FOX_WIDTH), lambda b: (b, 0, 0)),
        out_shape=jax.ShapeDtypeStruct((B, S, FOX_WIDTH), BF16),
        scratch_shapes=[pltpu.VMEM((2, FOX_HEADS, V_ROWS, t), F32)],
        compiler_params=_params("arbitrary"),
        name="fox_attn",
    )(qat, ka, vt)


def _pool_fold_kernel(pw_ref, ps_ref, wpo_ref, out_ref):
    for g in range(len(POOL_WINDOWS)):
        rows = slice(g * POOL_GC, (g + 1) * POOL_GC)
        scaled = pw_ref[g] * ps_ref[:, rows]
        out_ref[rows, :] = jnp.dot(scaled, wpo_ref[rows, :], preferred_element_type=F32,
                                   precision=lax.Precision.HIGHEST).astype(BF16)


def _pool_fold(pool_w, pool_scale, w_po):
    return pl.pallas_call(
        _pool_fold_kernel,
        out_shape=jax.ShapeDtypeStruct(w_po.shape, BF16),
        name="pool_fold",
    )(pool_w, pool_scale, w_po)


def _post_kernel(x_ref, u_ref, o_ref, kv_ref, gmix_ref, wg_ref, bg_ref, wpool_ref,
                 wfo_ref, wo_ref, gx_ref, wxq_ref, wxo_ref, x2_ref, ubuf_ref):
    tm = x_ref.shape[1]
    D = x_ref.shape[2]
    si = pl.program_id(1)
    rows = tm // POST_STREAMS
    streams = [slice(r * rows, (r + 1) * rows) for r in range(POST_STREAMS)]
    nt_dims = (((1,), (1,)), ((), ()))
    dot = functools.partial(jnp.dot, preferred_element_type=F32)

    @pl.when(si == 0)
    def _():
        ubuf_ref[0:MAX_WINDOW, :] = jnp.zeros((MAX_WINDOW, POOL_WIDTH), F32)

    @pl.when(si > 0)
    def _():
        ubuf_ref[0:MAX_WINDOW, :] = ubuf_ref[tm:tm + MAX_WINDOW, :]

    ubuf_ref[MAX_WINDOW:, :] = u_ref[0]

    def pool_delta(r, g):
        w = POOL_WINDOWS[g]
        sl = slice(g * POOL_GC, (g + 1) * POOL_GC)
        ws = ubuf_ref[r.start:r.stop + MAX_WINDOW, sl]
        k = 1
        while k < w:
            ws = ws + pltpu.roll(ws, k, 0)
            k *= 2
        pos = si * tm + r.start + lax.broadcasted_iota(jnp.int32, (rows, 1), 0)
        cnt = jnp.minimum(pos + 1, w).astype(F32)
        return (ws[MAX_WINDOW:, :] / cnt - u_ref[0, r, sl]).astype(BF16)

    y_pools = [dot(jnp.concatenate([pool_delta(r, g) for g in range(len(POOL_WINDOWS))], axis=1),
                   wpool_ref[...]) for r in streams]

    xs = [x_ref[0, r, :] for r in streams]
    hs = [_rmsnorm(x, gmix_ref[...]).astype(BF16) for x in xs]
    gates = [_sigmoid(dot(h, wg_ref[...]) + bg_ref[...]) for h in hs]
    y_foxs = [dot(o_ref[0, r, :], wfo_ref[...]) for r in streams]
    mixes = [(g[:, :D] * yp + g[:, D:] * yf).astype(BF16)
             for g, yp, yf in zip(gates, y_pools, y_foxs)]
    x1s = [x + dot(m, wo_ref[...]) for x, m in zip(xs, mixes)]

    hxs = [_rmsnorm(x1, gx_ref[...]).astype(BF16) for x1 in x1s]
    qxs = [dot(hx, wxq_ref[...]).astype(BF16) for hx in hxs]
    scores = [[lax.dot_general(qx[:, hd * X_DH:(hd + 1) * X_DH],
                               kv_ref[0, :, hd * X_DH:(hd + 1) * X_DH], nt_dims,
                               preferred_element_type=F32) * (X_DH ** -0.5)
               for hd in range(X_HEADS)] for qx in qxs]
    oxs = []
    for sc in scores:
        heads = []
        for hd, s in enumerate(sc):
            p = jnp.exp(s - jnp.max(s, axis=1, keepdims=True))
            l = jnp.sum(p, axis=1, keepdims=True)
            vh = kv_ref[0, :, X_WIDTH + hd * X_DH:X_WIDTH + (hd + 1) * X_DH]
            heads.append(dot(p.astype(BF16), vh) / l)
        oxs.append(jnp.concatenate(heads, axis=1).astype(BF16))
    for r, x1, ox in zip(streams, x1s, oxs):
        x2_ref[0, r, :] = x1 + dot(ox, wxo_ref[...])


def _post(x, u, o, kv, gmix, w_g, b_g, w_pool, w_fo, w_o, gx, w_xq, w_xo, tm):
    B, S, D = x.shape
    M = kv.shape[1]
    row = lambda n: pl.BlockSpec((1, tm, n), lambda b, s: (b, s, 0))
    consts = [gmix, w_g, b_g, w_pool, w_fo, w_o, gx, w_xq, w_xo]
    return pl.pallas_call(
        _post_kernel,
        grid=(B, S // tm),
        in_specs=[row(D), row(POOL_WIDTH), row(FOX_WIDTH),
                  pl.BlockSpec((1, M, 2 * X_WIDTH), lambda b, s: (b, 0, 0))]
                 + [_const_spec(a.shape) for a in consts],
        out_specs=row(D),
        out_shape=jax.ShapeDtypeStruct((B, S, D), F32),
        scratch_shapes=[pltpu.VMEM((tm + MAX_WINDOW, POOL_WIDTH), F32)],
        compiler_params=_params("arbitrary", "arbitrary"),
        name="post",
    )(x, u, o, kv, *consts)


def _ffn_kernel(x_ref, g_ref, wi_ref, wo_ref, gfin_ref, out_ref, *, final_norm):
    d_ff = wo_ref.shape[0]
    rows = x_ref.shape[0] // FFN_STREAMS
    streams = [slice(r * rows, (r + 1) * rows) for r in range(FFN_STREAMS)]
    dot = functools.partial(jnp.dot, preferred_element_type=F32)
    xs = [x_ref[r, :] for r in streams]
    hfs = [_rmsnorm(x, g_ref[...]).astype(BF16) for x in xs]
    accs = [None] * FFN_STREAMS
    for c0 in range(0, d_ff, FFN_CHUNK):
        c1 = min(c0 + FFN_CHUNK, d_ff)
        gts = [dot(hf, wi_ref[:, c0:c1]) for hf in hfs]
        ups = [dot(hf, wi_ref[:, d_ff + c0:d_ff + c1]) for hf in hfs]
        for i, (gt, up) in enumerate(zip(gts, ups)):
            part = dot((gt * _sigmoid(gt) * up).astype(BF16), wo_ref[c0:c1, :])
            accs[i] = part if accs[i] is None else accs[i] + part
    for r, x, acc in zip(streams, xs, accs):
        x3 = x + acc
        out_ref[r, :] = _rmsnorm(x3, gfin_ref[...]) if final_norm else x3


def _ffn(x2, g, w_i, w_o, g_fin, tm, final_norm):
    T, D = x2.shape
    return pl.pallas_call(
        functools.partial(_ffn_kernel, final_norm=final_norm),
        grid=(T // tm,),
        in_specs=[pl.BlockSpec((tm, D), lambda i: (i, 0)), _const_spec((1, D)),
                  pl.BlockSpec(w_i.shape, lambda i: (0, 0), pipeline_mode=pl.Buffered(1)),
                  pl.BlockSpec(w_o.shape, lambda i: (0, 0), pipeline_mode=pl.Buffered(1)),
                  _const_spec((1, D))],
        out_specs=pl.BlockSpec((tm, D), lambda i: (i, 0)),
        out_shape=jax.ShapeDtypeStruct((T, D), F32),
        compiler_params=_params("arbitrary"),
        name="ffn",
    )(x2, g, w_i, w_o, g_fin)


def kernel(x, mem, norm_mix_g, w_in, b_forget, b_gate, pool_w, pool_scale, w_pool_out, w_fox_out,
           w_out, norm_x_g, norm_mem_g, w_xq, w_xkv, w_xo, norm_ffn_g, w_ffn_in, w_ffn_out,
           norm_final_g):
    B, S, D = x.shape
    depth = w_in.shape[0]
    qkv_end = POOL_WIDTH + 3 * FOX_WIDTH
    f_end = qkv_end + FOX_HEADS
    row2 = lambda a: a.reshape(1, -1)
    for l in range(depth):
        wl = w_in[l]
        w_u, w_qkv, w_g = _split_cast(wl.T, WEIGHT_ROWS)
        rep = AUG_TERMS * FOX_HEADS
        w_f = jnp.pad(jnp.repeat(wl[:, qkv_end:f_end], AUG_TERMS, axis=1),
                      ((0, 0), (0, LANES - rep))).astype(BF16)
        b_f = jnp.pad(jnp.repeat(b_forget[l], AUG_TERMS), (0, LANES - rep)).reshape(1, LANES)
        u, qat, ka, vt, kv = _in_proj(x, row2(norm_mix_g[l]), w_u, w_qkv, w_f, b_f,
                                     mem, row2(norm_mem_g[l]), w_xkv[l].astype(BF16),
                                     IN_TILE, ATT_TILE)
        o = _fox_attention(qat, ka, vt, ATT_TILE)
        x2 = _post(x, u, o, kv, row2(norm_mix_g[l]), w_g, row2(b_gate[l]),
                   _pool_fold(pool_w[l], row2(pool_scale[l]), w_pool_out[l]),
                   w_fox_out[l].astype(BF16), w_out[l].astype(BF16), row2(norm_x_g[l]),
                   w_xq[l].astype(BF16), w_xo[l].astype(BF16), POST_TILE)
        x = _ffn(x2.reshape(B * S, D), row2(norm_ffn_g[l]), w_ffn_in[l].astype(BF16),
                 w_ffn_out[l].astype(BF16), row2(norm_final_g), FFN_TILE,
                 final_norm=(l == depth - 1)).reshape(B, S, D)
    return x
```

```python
import functools

import jax
import jax.numpy as jnp
from jax import lax
from jax.experimental import pallas as pl
from jax.experimental.pallas import tpu as pltpu

F32 = jnp.float32
BF16 = jnp.bfloat16

EPS = 1e-6
POOL_WINDOWS = (2, 4, 8, 16)
POOL_GC = 128
POOL_WIDTH = 512
MAX_WINDOW = 16
FOX_HEADS = 8
FOX_DH = 64
FOX_WIDTH = 512
X_HEADS = 4
X_DH = 128
X_WIDTH = 512
LANES = 128
MXU_COLS = 256
VMEM_LIMIT = 56 * 1024 * 1024
NEG_BIG = -1e30
AUG_TERMS = 3
LOG2_E = 1.4426950408889634
V_ROWS = FOX_DH + 16

ATT_TILE = 256
SCORE_LOOKAHEAD = 8
IN_TILE = 1024
IN_STREAMS = 2
POST_TILE = 1024
POST_STREAMS = 2
FFN_TILE = 1024
FFN_STREAMS = 2
FFN_CHUNK = 1024
WEIGHT_ROWS = 256


def _rmsnorm(x, g):
    return x * lax.rsqrt(jnp.mean(x * x, axis=-1, keepdims=True) + EPS) * g


def _log_sigmoid(x):
    return jnp.minimum(x, 0.0) - jnp.log1p(jnp.exp(-jnp.abs(x)))


def _sigmoid(x):
    return 1.0 / (1.0 + jnp.exp(-x))


def _const_spec(shape):
    return pl.BlockSpec(shape, lambda *_: (0,) * len(shape), pipeline_mode=pl.Buffered(1))


def _params(*sem):
    return pltpu.CompilerParams(dimension_semantics=sem, vmem_limit_bytes=VMEM_LIMIT)


def _split_cast_kernel(wt_ref, wu_ref, wqkv_ref, wg_ref):
    qkv_end = POOL_WIDTH + 3 * FOX_WIDTH
    wu_ref[...] = wt_ref[:POOL_WIDTH, :].T.astype(BF16)
    wqkv_ref[...] = wt_ref[POOL_WIDTH:qkv_end, :].T.astype(BF16)
    wg_ref[...] = wt_ref[qkv_end + FOX_HEADS:, :].T.astype(BF16)


def _split_cast(wt, rows):
    cols, D = wt.shape
    qkv_end = POOL_WIDTH + 3 * FOX_WIDTH
    n_gate = cols - qkv_end - FOX_HEADS
    blk = lambda n: pl.BlockSpec((rows, n), lambda i: (i, 0))
    return pl.pallas_call(
        _split_cast_kernel,
        grid=(D // rows,),
        in_specs=[pl.BlockSpec((cols, rows), lambda i: (0, i))],
        out_specs=[blk(POOL_WIDTH), blk(3 * FOX_WIDTH), blk(n_gate)],
        out_shape=[jax.ShapeDtypeStruct((D, POOL_WIDTH), BF16),
                   jax.ShapeDtypeStruct((D, 3 * FOX_WIDTH), BF16),
                   jax.ShapeDtypeStruct((D, n_gate), BF16)],
        compiler_params=_params("arbitrary"),
        name="split_cast",
    )(wt)


def _split3(c):
    hi = c.astype(BF16).astype(F32)
    r = c - hi
    lo = r.astype(BF16).astype(F32)
    return hi, lo, r - lo


def _in_proj_kernel(x_ref, g_ref, wu_ref, wqkv_ref, wf_ref, bf_ref, mem_ref, gmem_ref, wxkv_ref,
                    u_ref, qat_ref, ka_ref, vt_ref, kv_ref, csum_ref):
    tm = x_ref.shape[1]
    rows = tm // IN_STREAMS
    streams = [slice(r * rows, (r + 1) * rows) for r in range(IN_STREAMS)]
    dot = functools.partial(jnp.dot, preferred_element_type=F32)

    @pl.when(pl.program_id(1) == 0)
    def _():
        csum_ref[...] = jnp.zeros_like(csum_ref)
        mem_n = _rmsnorm(mem_ref[0], gmem_ref[...]).astype(BF16)
        kv_ref[0] = dot(mem_n, wxkv_ref[...]).astype(BF16)

    hs = [_rmsnorm(x_ref[0, r, :], g_ref[...]).astype(BF16) for r in streams]

    logfs = [_log_sigmoid(dot(h, wf_ref[...]) + bf_ref[...]) for h in hs]

    ta = vt_ref.shape[3]
    ones_rows = jnp.where(lax.broadcasted_iota(jnp.int32, (V_ROWS - FOX_DH, ta), 0) == 0,
                          1.0, 0.0).astype(BF16)
    for i, h in enumerate(hs):
        v = dot(h, wqkv_ref[:, 2 * FOX_WIDTH:])
        for sub in range(rows // ta):
            v_t = v[sub * ta:(sub + 1) * ta, :].T.astype(BF16)
            tile = i * (rows // ta) + sub
            for hd in range(FOX_HEADS):
                vt_ref[0, tile, hd * V_ROWS:hd * V_ROWS + FOX_DH, :] = v_t[hd * FOX_DH:(hd + 1) * FOX_DH, :]
                vt_ref[0, tile, hd * V_ROWS + FOX_DH:(hd + 1) * V_ROWS, :] = ones_rows

    row = lax.broadcasted_iota(jnp.int32, (rows, rows), 0)
    col = lax.broadcasted_iota(jnp.int32, (rows, rows), 1)
    tri = jnp.where(row >= col, 1.0, 0.0).astype(BF16)
    parts = [dot(tri, jnp.concatenate(_split3(logf), axis=1).astype(BF16)) for logf in logfs]
    lane = lax.broadcasted_iota(jnp.int32, (1, LANES), 1)
    piece = lane % AUG_TERMS
    c3s = []
    carry = csum_ref[...]
    for p in parts:
        c = p[:, :LANES] + p[:, LANES:2 * LANES] + p[:, 2 * LANES:] + carry
        carry = c[rows - 1:rows, :]
        c_hi, c_lo, c_lo2 = _split3(c * LOG2_E)
        c3s.append(jnp.where(piece == 0, c_hi, jnp.where(piece == 1, c_lo, c_lo2)))
    csum_ref[...] = carry

    heads_per_chunk = MXU_COLS // FOX_DH
    for chunk in range(FOX_WIDTH // MXU_COLS):
        cols = slice(chunk * MXU_COLS, (chunk + 1) * MXU_COLS)
        kcols = slice(FOX_WIDTH + chunk * MXU_COLS, FOX_WIDTH + (chunk + 1) * MXU_COLS)
        qcs = [dot(h, wqkv_ref[:, cols]) * (FOX_DH ** -0.5 * LOG2_E) for h in hs]
        kcs = [dot(h, wqkv_ref[:, kcols]) for h in hs]
        for r, c3, qc, kc in zip(streams, c3s, qcs, kcs):
            c3_neg = -c3
            for sub_head in range(heads_per_chunk):
                head = chunk * heads_per_chunk + sub_head
                pr, hh = divmod(sub_head, 2)
                base = FOX_DH if hh == 0 else 0
                a = lane - base
                in_aug = (a >= 0) & (a < FOX_DH)
                rq = pltpu.roll(c3, (base - AUG_TERMS * head) % LANES, 1)
                rk = pltpu.roll(c3_neg, (base + AUG_TERMS - AUG_TERMS * head) % LANES, 1)
                q_fill = jnp.where((a >= AUG_TERMS) & (a < 2 * AUG_TERMS), 1.0, 0.0)
                k_fill = jnp.where((a >= 0) & (a < AUG_TERMS), 1.0, 0.0)
                qp = qc[:, pr * LANES:(pr + 1) * LANES]
                kp = kc[:, pr * LANES:(pr + 1) * LANES]
                hs_ = slice(head * LANES, (head + 1) * LANES)
                qat_ref[0, head, :, r] = jnp.where((a >= 0) & (a < AUG_TERMS), rq,
                                                   jnp.where(in_aug, q_fill, qp)).T.astype(BF16)
                ka_ref[0, r, hs_] = jnp.where((a >= AUG_TERMS) & (a < 2 * AUG_TERMS), rk,
                                              jnp.where(in_aug, k_fill, kp)).astype(BF16)

    for r, h in zip(streams, hs):
        u_ref[0, r, :] = dot(h, wu_ref[...])


def _in_proj(x, g, w_u, w_qkv, w_f, b_f, mem, g_mem, w_xkv, tm, ta):
    B, S, D = x.shape
    M = mem.shape[1]
    row = lambda n: pl.BlockSpec((1, tm, n), lambda b, s: (b, s, 0))
    per_seq = lambda n: pl.BlockSpec((1, M, n), lambda b, s: (b, 0, 0))
    return pl.pallas_call(
        _in_proj_kernel,
        grid=(B, S // tm),
        in_specs=[row(D), _const_spec((1, D)), _const_spec(w_u.shape), _const_spec(w_qkv.shape),
                  _const_spec(w_f.shape), _const_spec((1, LANES)),
                  per_seq(D), _const_spec((1, D)), _const_spec(w_xkv.shape)],
        out_specs=[row(POOL_WIDTH),
                   pl.BlockSpec((1, FOX_HEADS, LANES, tm), lambda b, s: (b, 0, 0, s)),
                   row(FOX_HEADS * LANES),
                   pl.BlockSpec((1, tm // ta, FOX_HEADS * V_ROWS, ta), lambda b, s: (b, s, 0, 0)),
                   per_seq(2 * X_WIDTH)],
        out_shape=[jax.ShapeDtypeStruct((B, S, POOL_WIDTH), F32),
                   jax.ShapeDtypeStruct((B, FOX_HEADS, LANES, S), BF16),
                   jax.ShapeDtypeStruct((B, S, FOX_HEADS * LANES), BF16),
                   jax.ShapeDtypeStruct((B, S // ta, FOX_HEADS * V_ROWS, ta), BF16),
                   jax.ShapeDtypeStruct((B, M, 2 * X_WIDTH), BF16)],
        scratch_shapes=[pltpu.VMEM((1, LANES), F32)],
        compiler_params=_params("arbitrary", "arbitrary"),
        name="in_proj",
    )(x, g, w_u, w_qkv, w_f, b_f, mem, g_mem, w_xkv)


def _fox_kernel(qt_ref, k_ref, vt_ref, o_ref, acc_ref):
    t = vt_ref.shape[3]
    n_tiles = vt_ref.shape[1]

    causal_t = (lax.broadcasted_iota(jnp.int32, (t, t), 0) <=
                lax.broadcasted_iota(jnp.int32, (t, t), 1))

    items = [(qi, j, hd) for qi in range(n_tiles) for j in range(qi + 1)
             for hd in range(FOX_HEADS)]

    def scores(idx):
        qi, j, hd = items[idx]
        return jnp.dot(k_ref[0, j * t:(j + 1) * t, hd * LANES:(hd + 1) * LANES],
                       qt_ref[0, hd, :, qi * t:(qi + 1) * t],
                       preferred_element_type=F32)

    pending = [scores(idx) for idx in range(SCORE_LOOKAHEAD)]
    ms = None
    for idx, (qi, j, hd) in enumerate(items):
        last = j == qi
        acc = acc_ref.at[qi % 2]
        if j == 0 and hd == 0:
            acc[...] = jnp.zeros_like(acc)
            ms = [jnp.full((1, t), NEG_BIG, F32) for _ in range(FOX_HEADS)]
        st = pending.pop(0)
        if idx + SCORE_LOOKAHEAD < len(items):
            pending.append(scores(idx + SCORE_LOOKAHEAD))
        if last:
            st = jnp.where(causal_t, st, NEG_BIG)
        m_new = jnp.maximum(ms[hd], jnp.max(st, axis=0, keepdims=True))
        alpha = jnp.exp2(ms[hd] - m_new)
        pt = jnp.exp2(st - m_new).astype(BF16)
        pv = jnp.dot(vt_ref[0, j, hd * V_ROWS:(hd + 1) * V_ROWS, :], pt,
                     preferred_element_type=F32)
        acc[hd] = alpha * acc[hd] + pv
        ms[hd] = m_new
        if last and hd == FOX_HEADS - 1:
            o_t = jnp.concatenate([acc[h, :FOX_DH, :] / acc[h, FOX_DH:FOX_DH + 1, :]
                                   for h in range(FOX_HEADS)], axis=0)
            o_ref[0, qi * t:(qi + 1) * t, :] = o_t.T.astype(BF16)


def _fox_attention(qat, ka, vt, t):
    B, S, _ = ka.shape
    return pl.pallas_call(
        _fox_kernel,
        grid=(B,),
        in_specs=[pl.BlockSpec((1, FOX_HEADS, LANES, S), lambda b: (b, 0, 0, 0)),
                  pl.BlockSpec((1, S, FOX_HEADS * LANES), lambda b: (b, 0, 0)),
                  pl.BlockSpec((1, S // t, FOX_HEADS * V_ROWS, t), lambda b: (b, 0, 0, 0))],
        out_specs=pl.BlockSpec((1, S, FOX_WIDTH), lambda b: (b, 0, 0)),
        out_shape=jax.ShapeDtypeStruct((B, S, FOX_WIDTH), BF16),
        scratch_shapes=[pltpu.VMEM((2, FOX_HEADS, V_ROWS, t), F32)],
        compiler_params=_params("arbitrary"),
        name="fox_attn",
    )(qat, ka, vt)


def _pool_fold_kernel(pw_ref, ps_ref, wpo_ref, out_ref):
    for g in range(len(POOL_WINDOWS)):
        rows = slice(g * POOL_GC, (g + 1) * POOL_GC)
        scaled = pw_ref[g] * ps_ref[:, rows]
        out_ref[rows, :] = jnp.dot(scaled, wpo_ref[rows, :], preferred_element_type=F32,
                                   precision=lax.Precision.HIGHEST).astype(BF16)


def _pool_fold(pool_w, pool_scale, w_po):
    return pl.pallas_call(
        _pool_fold_kernel,
        out_shape=jax.ShapeDtypeStruct(w_po.shape, BF16),
        name="pool_fold",
    )(pool_w, pool_scale, w_po)


def _post_kernel(x_ref, u_ref, o_ref, kv_ref, gmix_ref, wg_ref, bg_ref, wpool_ref,
                 wfo_ref, wo_ref, gx_ref, wxq_ref, wxo_ref, x2_ref, ubuf_ref):
    tm = x_ref.shape[1]
    D = x_ref.shape[2]
    si = pl.program_id(1)
    rows = tm // POST_STREAMS
    streams = [slice(r * rows, (r + 1) * rows) for r in range(POST_STREAMS)]
    nt_dims = (((1,), (1,)), ((), ()))
    dot = functools.partial(jnp.dot, preferred_element_type=F32)

    @pl.when(si == 0)
    def _():
        ubuf_ref[0:MAX_WINDOW, :] = jnp.zeros((MAX_WINDOW, POOL_WIDTH), F32)

    @pl.when(si > 0)
    def _():
        ubuf_ref[0:MAX_WINDOW, :] = ubuf_ref[tm:tm + MAX_WINDOW, :]

    ubuf_ref[MAX_WINDOW:, :] = u_ref[0]

    def pool_delta(r, g):
        w = POOL_WINDOWS[g]
        sl = slice(g * POOL_GC, (g + 1) * POOL_GC)
        ws = ubuf_ref[r.start:r.stop + MAX_WINDOW, sl]
        k = 1
        while k < w:
            ws = ws + pltpu.roll(ws, k, 0)
            k *= 2
        pos = si * tm + r.start + lax.broadcasted_iota(jnp.int32, (rows, 1), 0)
        cnt = jnp.minimum(pos + 1, w).astype(F32)
        return (ws[MAX_WINDOW:, :] / cnt - u_ref[0, r, sl]).astype(BF16)

    y_pools = [dot(jnp.concatenate([pool_delta(r, g) for g in range(len(POOL_WINDOWS))], axis=1),
                   wpool_ref[...]) for r in streams]

    xs = [x_ref[0, r, :] for r in streams]
    hs = [_rmsnorm(x, gmix_ref[...]).astype(BF16) for x in xs]
    gates = [_sigmoid(dot(h, wg_ref[...]) + bg_ref[...]) for h in hs]
    y_foxs = [dot(o_ref[0, r, :], wfo_ref[...]) for r in streams]
    mixes = [(g[:, :D] * yp + g[:, D:] * yf).astype(BF16)
             for g, yp, yf in zip(gates, y_pools, y_foxs)]
    x1s = [x + dot(m, wo_ref[...]) for x, m in zip(xs, mixes)]

    hxs = [_rmsnorm(x1, gx_ref[...]).astype(BF16) for x1 in x1s]
    qxs = [dot(hx, wxq_ref[...]).astype(BF16) for hx in hxs]
    scores = [[lax.dot_general(qx[:, hd * X_DH:(hd + 1) * X_DH],
                               kv_ref[0, :, hd * X_DH:(hd + 1) * X_DH], nt_dims,
                               preferred_element_type=F32) * (X_DH ** -0.5)
               for hd in range(X_HEADS)] for qx in qxs]
    oxs = []
    for sc in scores:
        heads = []
        for hd, s in enumerate(sc):
            p = jnp.exp(s - jnp.max(s, axis=1, keepdims=True))
            l = jnp.sum(p, axis=1, keepdims=True)
            vh = kv_ref[0, :, X_WIDTH + hd * X_DH:X_WIDTH + (hd + 1) * X_DH]
            heads.append(dot(p.astype(BF16), vh) / l)
        oxs.append(jnp.concatenate(heads, axis=1).astype(BF16))
    for r, x1, ox in zip(streams, x1s, oxs):
        x2_ref[0, r, :] = x1 + dot(ox, wxo_ref[...])


def _post(x, u, o, kv, gmix, w_g, b_g, w_pool, w_fo, w_o, gx, w_xq, w_xo, tm):
    B, S, D = x.shape
    M = kv.shape[1]
    row = lambda n: pl.BlockSpec((1, tm, n), lambda b, s: (b, s, 0))
    consts = [gmix, w_g, b_g, w_pool, w_fo, w_o, gx, w_xq, w_xo]
    return pl.pallas_call(
        _post_kernel,
        grid=(B, S // tm),
        in_specs=[row(D), row(POOL_WIDTH), row(FOX_WIDTH),
                  pl.BlockSpec((1, M, 2 * X_WIDTH), lambda b, s: (b, 0, 0))]
                 + [_const_spec(a.shape) for a in consts],
        out_specs=row(D),
        out_shape=jax.ShapeDtypeStruct((B, S, D), F32),
        scratch_shapes=[pltpu.VMEM((tm + MAX_WINDOW, POOL_WIDTH), F32)],
        compiler_params=_params("arbitrary", "arbitrary"),
        name="post",
    )(x, u, o, kv, *consts)


def _ffn_kernel(x_ref, g_ref, wi_ref, wo_ref, gfin_ref, out_ref, *, final_norm):
    d_ff = wo_ref.shape[0]
    rows = x_ref.shape[0] // FFN_STREAMS
    streams = [slice(r * rows, (r + 1) * rows) for r in range(FFN_STREAMS)]
    dot = functools.partial(jnp.dot, preferred_element_type=F32)
    xs = [x_ref[r, :] for r in streams]
    hfs = [_rmsnorm(x, g_ref[...]).astype(BF16) for x in xs]
    accs = [None] * FFN_STREAMS
    for c0 in range(0, d_ff, FFN_CHUNK):
        c1 = min(c0 + FFN_CHUNK, d_ff)
        gts = [dot(hf, wi_ref[:, c0:c1]) for hf in hfs]
        ups = [dot(hf, wi_ref[:, d_ff + c0:d_ff + c1]) for hf in hfs]
        for i, (gt, up) in enumerate(zip(gts, ups)):
            part = dot((gt * _sigmoid(gt) * up).astype(BF16), wo_ref[c0:c1, :])
            accs[i] = part if accs[i] is None else accs[i] + part
    for r, x, acc in zip(streams, xs, accs):
        x3 = x + acc
        out_ref[r, :] = _rmsnorm(x3, gfin_ref[...]) if final_norm else x3


def _ffn(x2, g, w_i, w_o, g_fin, tm, final_norm):
    T, D = x2.shape
    return pl.pallas_call(
        functools.partial(_ffn_kernel, final_norm=final_norm),
        grid=(T // tm,),
        in_specs=[pl.BlockSpec((tm, D), lambda i: (i, 0)), _const_spec((1, D)),
                  _const_spec(w_i.shape), _const_spec(w_o.shape), _const_spec((1, D))],
        out_specs=pl.BlockSpec((tm, D), lambda i: (i, 0)),
        out_shape=jax.ShapeDtypeStruct((T, D), F32),
        compiler_params=_params("arbitrary"),
        name="ffn",
    )(x2, g, w_i, w_o, g_fin)


def kernel(x, mem, norm_mix_g, w_in, b_forget, b_gate, pool_w, pool_scale, w_pool_out, w_fox_out,
           w_out, norm_x_g, norm_mem_g, w_xq, w_xkv, w_xo, norm_ffn_g, w_ffn_in, w_ffn_out,
           norm_final_g):
    B, S, D = x.shape
    depth = w_in.shape[0]
    qkv_end = POOL_WIDTH + 3 * FOX_WIDTH
    f_end = qkv_end + FOX_HEADS
    row2 = lambda a: a.reshape(1, -1)
    for l in range(depth):
        wl = w_in[l]
        w_u, w_qkv, w_g = _split_cast(wl.T, WEIGHT_ROWS)
        rep = AUG_TERMS * FOX_HEADS
        w_f = jnp.pad(jnp.repeat(wl[:, qkv_end:f_end], AUG_TERMS, axis=1),
                      ((0, 0), (0, LANES - rep))).astype(BF16)
        b_f = jnp.pad(jnp.repeat(b_forget[l], AUG_TERMS), (0, LANES - rep)).reshape(1, LANES)
        u, qat, ka, vt, kv = _in_proj(x, row2(norm_mix_g[l]), w_u, w_qkv, w_f, b_f,
                                     mem, row2(norm_mem_g[l]), w_xkv[l].astype(BF16),
                                     IN_TILE, ATT_TILE)
        o = _fox_attention(qat, ka, vt, ATT_TILE)
        x2 = _post(x, u, o, kv, row2(norm_mix_g[l]), w_g, row2(b_gate[l]),
                   _pool_fold(pool_w[l], row2(pool_scale[l]), w_pool_out[l]),
                   w_fox_out[l].astype(BF16), w_out[l].astype(BF16), row2(norm_x_g[l]),
                   w_xq[l].astype(BF16), w_xo[l].astype(BF16), POST_TILE)
        x = _ffn(x2.reshape(B * S, D), row2(norm_ffn_g[l]), w_ffn_in[l].astype(BF16),
                 w_ffn_out[l].astype(BF16), row2(norm_final_g), FFN_TILE,
                 final_norm=(l == depth - 1)).reshape(B, S, D)
    return x
```

```python
import functools

import jax
import jax.numpy as jnp
from jax import lax
from jax.experimental import pallas as pl
from jax.experimental.pallas import tpu as pltpu

F32 = jnp.float32
BF16 = jnp.bfloat16

EPS = 1e-6
POOL_WINDOWS = (2, 4, 8, 16)
POOL_GC = 128
POOL_WIDTH = 512
MAX_WINDOW = 16
FOX_HEADS = 8
FOX_DH = 64
FOX_WIDTH = 512
X_HEADS = 4
X_DH = 128
X_WIDTH = 512
LANES = 128
MXU_COLS = 256
VMEM_LIMIT = 56 * 1024 * 1024
NEG_BIG = -1e30
AUG_TERMS = 3
LOG2_E = 1.4426950408889634
V_ROWS = FOX_DH + 16

ATT_TILE = 256
SCORE_LOOKAHEAD = 6
IN_TILE = 1024
IN_STREAMS = 2
POST_TILE = 1024
POST_STREAMS = 2
FFN_TILE = 1024
FFN_STREAMS = 2
FFN_CHUNK = 1024
WEIGHT_ROWS = 256


def _rmsnorm(x, g):
    return x * lax.rsqrt(jnp.mean(x * x, axis=-1, keepdims=True) + EPS) * g


def _log_sigmoid(x):
    return jnp.minimum(x, 0.0) - jnp.log1p(jnp.exp(-jnp.abs(x)))


def _sigmoid(x):
    return 1.0 / (1.0 + jnp.exp(-x))


def _const_spec(shape):
    return pl.BlockSpec(shape, lambda *_: (0,) * len(shape), pipeline_mode=pl.Buffered(1))


def _params(*sem):
    return pltpu.CompilerParams(dimension_semantics=sem, vmem_limit_bytes=VMEM_LIMIT)


def _split_cast_kernel(wt_ref, wu_ref, wqkv_ref, wg_ref):
    qkv_end = POOL_WIDTH + 3 * FOX_WIDTH
    wu_ref[...] = wt_ref[:POOL_WIDTH, :].T.astype(BF16)
    wqkv_ref[...] = wt_ref[POOL_WIDTH:qkv_end, :].T.astype(BF16)
    wg_ref[...] = wt_ref[qkv_end + FOX_HEADS:, :].T.astype(BF16)


def _split_cast(wt, rows):
    cols, D = wt.shape
    qkv_end = POOL_WIDTH + 3 * FOX_WIDTH
    n_gate = cols - qkv_end - FOX_HEADS
    blk = lambda n: pl.BlockSpec((rows, n), lambda i: (i, 0))
    return pl.pallas_call(
        _split_cast_kernel,
        grid=(D // rows,),
        in_specs=[pl.BlockSpec((cols, rows), lambda i: (0, i))],
        out_specs=[blk(POOL_WIDTH), blk(3 * FOX_WIDTH), blk(n_gate)],
        out_shape=[jax.ShapeDtypeStruct((D, POOL_WIDTH), BF16),
                   jax.ShapeDtypeStruct((D, 3 * FOX_WIDTH), BF16),
                   jax.ShapeDtypeStruct((D, n_gate), BF16)],
        compiler_params=_params("arbitrary"),
        name="split_cast",
    )(wt)


def _split3(c):
    hi = c.astype(BF16).astype(F32)
    r = c - hi
    lo = r.astype(BF16).astype(F32)
    return hi, lo, r - lo


def _in_proj_kernel(x_ref, g_ref, wu_ref, wqkv_ref, wf_ref, bf_ref, mem_ref, gmem_ref, wxkv_ref,
                    u_ref, qat_ref, ka_ref, vt_ref, kv_ref, csum_ref):
    tm = x_ref.shape[1]
    rows = tm // IN_STREAMS
    streams = [slice(r * rows, (r + 1) * rows) for r in range(IN_STREAMS)]
    dot = functools.partial(jnp.dot, preferred_element_type=F32)

    @pl.when(pl.program_id(1) == 0)
    def _():
        csum_ref[...] = jnp.zeros_like(csum_ref)
        mem_n = _rmsnorm(mem_ref[0], gmem_ref[...]).astype(BF16)
        kv_ref[0] = dot(mem_n, wxkv_ref[...]).astype(BF16)

    hs = [_rmsnorm(x_ref[0, r, :], g_ref[...]).astype(BF16) for r in streams]

    logfs = [_log_sigmoid(dot(h, wf_ref[...]) + bf_ref[...]) for h in hs]

    ta = vt_ref.shape[3]
    ones_rows = jnp.where(lax.broadcasted_iota(jnp.int32, (V_ROWS - FOX_DH, ta), 0) == 0,
                          1.0, 0.0).astype(BF16)
    for i, h in enumerate(hs):
        v = dot(h, wqkv_ref[:, 2 * FOX_WIDTH:])
        for sub in range(rows // ta):
            v_t = v[sub * ta:(sub + 1) * ta, :].T.astype(BF16)
            tile = i * (rows // ta) + sub
            for hd in range(FOX_HEADS):
                vt_ref[0, tile, hd * V_ROWS:hd * V_ROWS + FOX_DH, :] = v_t[hd * FOX_DH:(hd + 1) * FOX_DH, :]
                vt_ref[0, tile, hd * V_ROWS + FOX_DH:(hd + 1) * V_ROWS, :] = ones_rows

    row = lax.broadcasted_iota(jnp.int32, (rows, rows), 0)
    col = lax.broadcasted_iota(jnp.int32, (rows, rows), 1)
    tri = jnp.where(row >= col, 1.0, 0.0).astype(BF16)
    parts = [dot(tri, jnp.concatenate(_split3(logf), axis=1).astype(BF16)) for logf in logfs]
    lane = lax.broadcasted_iota(jnp.int32, (1, LANES), 1)
    piece = lane % AUG_TERMS
    c3s = []
    carry = csum_ref[...]
    for p in parts:
        c = p[:, :LANES] + p[:, LANES:2 * LANES] + p[:, 2 * LANES:] + carry
        carry = c[rows - 1:rows, :]
        c_hi, c_lo, c_lo2 = _split3(c * LOG2_E)
        c3s.append(jnp.where(piece == 0, c_hi, jnp.where(piece == 1, c_lo, c_lo2)))
    csum_ref[...] = carry

    heads_per_chunk = MXU_COLS // FOX_DH
    for chunk in range(FOX_WIDTH // MXU_COLS):
        cols = slice(chunk * MXU_COLS, (chunk + 1) * MXU_COLS)
        kcols = slice(FOX_WIDTH + chunk * MXU_COLS, FOX_WIDTH + (chunk + 1) * MXU_COLS)
        qcs = [dot(h, wqkv_ref[:, cols]) * (FOX_DH ** -0.5 * LOG2_E) for h in hs]
        kcs = [dot(h, wqkv_ref[:, kcols]) for h in hs]
        for r, c3, qc, kc in zip(streams, c3s, qcs, kcs):
            c3_neg = -c3
            for sub_head in range(heads_per_chunk):
                head = chunk * heads_per_chunk + sub_head
                pr, hh = divmod(sub_head, 2)
                base = FOX_DH if hh == 0 else 0
                a = lane - base
                in_aug = (a >= 0) & (a < FOX_DH)
                rq = pltpu.roll(c3, (base - AUG_TERMS * head) % LANES, 1)
                rk = pltpu.roll(c3_neg, (base + AUG_TERMS - AUG_TERMS * head) % LANES, 1)
                q_fill = jnp.where((a >= AUG_TERMS) & (a < 2 * AUG_TERMS), 1.0, 0.0)
                k_fill = jnp.where((a >= 0) & (a < AUG_TERMS), 1.0, 0.0)
                qp = qc[:, pr * LANES:(pr + 1) * LANES]
                kp = kc[:, pr * LANES:(pr + 1) * LANES]
                hs_ = slice(head * LANES, (head + 1) * LANES)
                qat_ref[0, head, :, r] = jnp.where((a >= 0) & (a < AUG_TERMS), rq,
                                                   jnp.where(in_aug, q_fill, qp)).T.astype(BF16)
                ka_ref[0, r, hs_] = jnp.where((a >= AUG_TERMS) & (a < 2 * AUG_TERMS), rk,
                                              jnp.where(in_aug, k_fill, kp)).astype(BF16)

    for r, h in zip(streams, hs):
        u_ref[0, r, :] = dot(h, wu_ref[...])


def _in_proj(x, g, w_u, w_qkv, w_f, b_f, mem, g_mem, w_xkv, tm, ta):
    B, S, D = x.shape
    M = mem.shape[1]
    row = lambda n: pl.BlockSpec((1, tm, n), lambda b, s: (b, s, 0))
    per_seq = lambda n: pl.BlockSpec((1, M, n), lambda b, s: (b, 0, 0))
    return pl.pallas_call(
        _in_proj_kernel,
        grid=(B, S // tm),
        in_specs=[row(D), _const_spec((1, D)), _const_spec(w_u.shape), _const_spec(w_qkv.shape),
                  _const_spec(w_f.shape), _const_spec((1, LANES)),
                  per_seq(D), _const_spec((1, D)), _const_spec(w_xkv.shape)],
        out_specs=[row(POOL_WIDTH),
                   pl.BlockSpec((1, FOX_HEADS, LANES, tm), lambda b, s: (b, 0, 0, s)),
                   row(FOX_HEADS * LANES),
                   pl.BlockSpec((1, tm // ta, FOX_HEADS * V_ROWS, ta), lambda b, s: (b, s, 0, 0)),
                   per_seq(2 * X_WIDTH)],
        out_shape=[jax.ShapeDtypeStruct((B, S, POOL_WIDTH), F32),
                   jax.ShapeDtypeStruct((B, FOX_HEADS, LANES, S), BF16),
                   jax.ShapeDtypeStruct((B, S, FOX_HEADS * LANES), BF16),
                   jax.ShapeDtypeStruct((B, S // ta, FOX_HEADS * V_ROWS, ta), BF16),
                   jax.ShapeDtypeStruct((B, M, 2 * X_WIDTH), BF16)],
        scratch_shapes=[pltpu.VMEM((1, LANES), F32)],
        compiler_params=_params("arbitrary", "arbitrary"),
        name="in_proj",
    )(x, g, w_u, w_qkv, w_f, b_f, mem, g_mem, w_xkv)


def _fox_kernel(qt_ref, k_ref, vt_ref, o_ref, acc_ref):
    t = vt_ref.shape[3]
    n_tiles = vt_ref.shape[1]

    causal_t = (lax.broadcasted_iota(jnp.int32, (t, t), 0) <=
                lax.broadcasted_iota(jnp.int32, (t, t), 1))

    items = [(qi, j, hd) for qi in range(n_tiles) for j in range(qi + 1)
             for hd in range(FOX_HEADS)]

    def scores(idx):
        qi, j, hd = items[idx]
        return jnp.dot(k_ref[0, j * t:(j + 1) * t, hd * LANES:(hd + 1) * LANES],
                       qt_ref[0, hd, :, qi * t:(qi + 1) * t],
                       preferred_element_type=F32)

    pending = [scores(idx) for idx in range(SCORE_LOOKAHEAD)]
    ms = None
    for idx, (qi, j, hd) in enumerate(items):
        last = j == qi
        acc = acc_ref.at[qi % 2]
        if j == 0 and hd == 0:
            acc[...] = jnp.zeros_like(acc)
            ms = [jnp.full((1, t), NEG_BIG, F32) for _ in range(FOX_HEADS)]
        st = pending.pop(0)
        if idx + SCORE_LOOKAHEAD < len(items):
            pending.append(scores(idx + SCORE_LOOKAHEAD))
        if last:
            st = jnp.where(causal_t, st, NEG_BIG)
        m_new = jnp.maximum(ms[hd], jnp.max(st, axis=0, keepdims=True))
        alpha = jnp.exp2(ms[hd] - m_new)
        pt = jnp.exp2(st - m_new).astype(BF16)
        pv = jnp.dot(vt_ref[0, j, hd * V_ROWS:(hd + 1) * V_ROWS, :], pt,
                     preferred_element_type=F32)
        acc[hd] = alpha * acc[hd] + pv
        ms[hd] = m_new
        if last and hd == FOX_HEADS - 1:
            o_t = jnp.concatenate([acc[h, :FOX_DH, :] / acc[h, FOX_DH:FOX_DH + 1, :]
                                   for h in range(FOX_HEADS)], axis=0)
            o_ref[0, qi * t:(qi + 1) * t, :] = o_t.T.astype(BF16)


def _fox_attention(qat, ka, vt, t):
    B, S, _ = ka.shape
    return pl.pallas_call(
        _fox_kernel,
        grid=(B,),
        in_specs=[pl.BlockSpec((1, FOX_HEADS, LANES, S), lambda b: (b, 0, 0, 0)),
                  pl.BlockSpec((1, S, FOX_HEADS * LANES), lambda b: (b, 0, 0)),
                  pl.BlockSpec((1, S // t, FOX_HEADS * V_ROWS, t), lambda b: (b, 0, 0, 0))],
        out_specs=pl.BlockSpec((1, S, FOX_WIDTH), lambda b: (b, 0, 0)),
        out_shape=jax.ShapeDtypeStruct((B, S, FOX_WIDTH), BF16),
        scratch_shapes=[pltpu.VMEM((2, FOX_HEADS, V_ROWS, t), F32)],
        compiler_params=_params("arbitrary"),
        name="fox_attn",
    )(qat, ka, vt)


def _pool_fold_kernel(pw_ref, ps_ref, wpo_ref, out_ref):
    for g in range(len(POOL_WINDOWS)):
        rows = slice(g * POOL_GC, (g + 1) * POOL_GC)
        scaled = pw_ref[g] * ps_ref[:, rows]
        out_ref[rows, :] = jnp.dot(scaled, wpo_ref[rows, :], preferred_element_type=F32,
                                   precision=lax.Precision.HIGHEST).astype(BF16)


def _pool_fold(pool_w, pool_scale, w_po):
    return pl.pallas_call(
        _pool_fold_kernel,
        out_shape=jax.ShapeDtypeStruct(w_po.shape, BF16),
        name="pool_fold",
    )(pool_w, pool_scale, w_po)


def _post_kernel(x_ref, u_ref, o_ref, kv_ref, gmix_ref, wg_ref, bg_ref, wpool_ref,
                 wfo_ref, wo_ref, gx_ref, wxq_ref, wxo_ref, x2_ref, ubuf_ref):
    tm = x_ref.shape[1]
    D = x_ref.shape[2]
    si = pl.program_id(1)
    rows = tm // POST_STREAMS
    streams = [slice(r * rows, (r + 1) * rows) for r in range(POST_STREAMS)]
    nt_dims = (((1,), (1,)), ((), ()))
    dot = functools.partial(jnp.dot, preferred_element_type=F32)

    @pl.when(si == 0)
    def _():
        ubuf_ref[0:MAX_WINDOW, :] = jnp.zeros((MAX_WINDOW, POOL_WIDTH), F32)

    @pl.when(si > 0)
    def _():
        ubuf_ref[0:MAX_WINDOW, :] = ubuf_ref[tm:tm + MAX_WINDOW, :]

    ubuf_ref[MAX_WINDOW:, :] = u_ref[0]

    def pool_delta(r, g):
        w = POOL_WINDOWS[g]
        sl = slice(g * POOL_GC, (g + 1) * POOL_GC)
        ws = ubuf_ref[r.start:r.stop + MAX_WINDOW, sl]
        k = 1
        while k < w:
            ws = ws + pltpu.roll(ws, k, 0)
            k *= 2
        pos = si * tm + r.start + lax.broadcasted_iota(jnp.int32, (rows, 1), 0)
        cnt = jnp.minimum(pos + 1, w).astype(F32)
        return (ws[MAX_WINDOW:, :] / cnt - u_ref[0, r, sl]).astype(BF16)

    y_pools = [dot(jnp.concatenate([pool_delta(r, g) for g in range(len(POOL_WINDOWS))], axis=1),
                   wpool_ref[...]) for r in streams]

    xs = [x_ref[0, r, :] for r in streams]
    hs = [_rmsnorm(x, gmix_ref[...]).astype(BF16) for x in xs]
    gates = [_sigmoid(dot(h, wg_ref[...]) + bg_ref[...]) for h in hs]
    y_foxs = [dot(o_ref[0, r, :], wfo_ref[...]) for r in streams]
    mixes = [(g[:, :D] * yp + g[:, D:] * yf).astype(BF16)
             for g, yp, yf in zip(gates, y_pools, y_foxs)]
    x1s = [x + dot(m, wo_ref[...]) for x, m in zip(xs, mixes)]

    hxs = [_rmsnorm(x1, gx_ref[...]).astype(BF16) for x1 in x1s]
    qxs = [dot(hx, wxq_ref[...]).astype(BF16) for hx in hxs]
    scores = [[lax.dot_general(qx[:, hd * X_DH:(hd + 1) * X_DH],
                               kv_ref[0, :, hd * X_DH:(hd + 1) * X_DH], nt_dims,
                               preferred_element_type=F32) * (X_DH ** -0.5)
               for hd in range(X_HEADS)] for qx in qxs]
    oxs = []
    for sc in scores:
        heads = []
        for hd, s in enumerate(sc):
            p = jnp.exp(s - jnp.max(s, axis=1, keepdims=True))
            l = jnp.sum(p, axis=1, keepdims=True)
            vh = kv_ref[0, :, X_WIDTH + hd * X_DH:X_WIDTH + (hd + 1) * X_DH]
            heads.append(dot(p.astype(BF16), vh) / l)
        oxs.append(jnp.concatenate(heads, axis=1).astype(BF16))
    for r, x1, ox in zip(streams, x1s, oxs):
        x2_ref[0, r, :] = x1 + dot(ox, wxo_ref[...])


def _post(x, u, o, kv, gmix, w_g, b_g, w_pool, w_fo, w_o, gx, w_xq, w_xo, tm):
    B, S, D = x.shape
    M = kv.shape[1]
    row = lambda n: pl.BlockSpec((1, tm, n), lambda b, s: (b, s, 0))
    consts = [gmix, w_g, b_g, w_pool, w_fo, w_o, gx, w_xq, w_xo]
    return pl.pallas_call(
        _post_kernel,
        grid=(B, S // tm),
        in_specs=[row(D), row(POOL_WIDTH), row(FOX_WIDTH),
                  pl.BlockSpec((1, M, 2 * X_WIDTH), lambda b, s: (b, 0, 0))]
                 + [_const_spec(a.shape) for a in consts],
        out_specs=row(D),
        out_shape=jax.ShapeDtypeStruct((B, S, D), F32),
        scratch_shapes=[pltpu.VMEM((tm + MAX_WINDOW, POOL_WIDTH), F32)],
        compiler_params=_params("arbitrary", "arbitrary"),
        name="post",
    )(x, u, o, kv, *consts)


def _ffn_kernel(x_ref, g_ref, wi_ref, wo_ref, gfin_ref, out_ref, *, final_norm):
    d_ff = wo_ref.shape[0]
    rows = x_ref.shape[0] // FFN_STREAMS
    streams = [slice(r * rows, (r + 1) * rows) for r in range(FFN_STREAMS)]
    dot = functools.partial(jnp.dot, preferred_element_type=F32)
    xs = [x_ref[r, :] for r in streams]
    hfs = [_rmsnorm(x, g_ref[...]).astype(BF16) for x in xs]
    accs = [None] * FFN_STREAMS
    for c0 in range(0, d_ff, FFN_CHUNK):
        c1 = min(c0 + FFN_CHUNK, d_ff)
        gts = [dot(hf, wi_ref[:, c0:c1]) for hf in hfs]
        ups = [dot(hf, wi_ref[:, d_ff + c0:d_ff + c1]) for hf in hfs]
        for i, (gt, up) in enumerate(zip(gts, ups)):
            part = dot((gt * _sigmoid(gt) * up).astype(BF16), wo_ref[c0:c1, :])
            accs[i] = part if accs[i] is None else accs[i] + part
    for r, x, acc in zip(streams, xs, accs):
        x3 = x + acc
        out_ref[r, :] = _rmsnorm(x3, gfin_ref[...]) if final_norm else x3


def _ffn(x2, g, w_i, w_o, g_fin, tm, final_norm):
    T, D = x2.shape
    return pl.pallas_call(
        functools.partial(_ffn_kernel, final_norm=final_norm),
        grid=(T // tm,),
        in_specs=[pl.BlockSpec((tm, D), lambda i: (i, 0)), _const_spec((1, D)),
                  _const_spec(w_i.shape), _const_spec(w_o.shape), _const_spec((1, D))],
        out_specs=pl.BlockSpec((tm, D), lambda i: (i, 0)),
        out_shape=jax.ShapeDtypeStruct((T, D), F32),
        compiler_params=_params("arbitrary"),
        name="ffn",
    )(x2, g, w_i, w_o, g_fin)


def kernel(x, mem, norm_mix_g, w_in, b_forget, b_gate, pool_w, pool_scale, w_pool_out, w_fox_out,
           w_out, norm_x_g, norm_mem_g, w_xq, w_xkv, w_xo, norm_ffn_g, w_ffn_in, w_ffn_out,
           norm_final_g):
    B, S, D = x.shape
    depth = w_in.shape[0]
    qkv_end = POOL_WIDTH + 3 * FOX_WIDTH
    f_end = qkv_end + FOX_HEADS
    row2 = lambda a: a.reshape(1, -1)
    for l in range(depth):
        wl = w_in[l]
        w_u, w_qkv, w_g = _split_cast(wl.T, WEIGHT_ROWS)
        rep = AUG_TERMS * FOX_HEADS
        w_f = jnp.pad(jnp.repeat(wl[:, qkv_end:f_end], AUG_TERMS, axis=1),
                      ((0, 0), (0, LANES - rep))).astype(BF16)
        b_f = jnp.pad(jnp.repeat(b_forget[l], AUG_TERMS), (0, LANES - rep)).reshape(1, LANES)
        u, qat, ka, vt, kv = _in_proj(x, row2(norm_mix_g[l]), w_u, w_qkv, w_f, b_f,
                                     mem, row2(norm_mem_g[l]), w_xkv[l].astype(BF16),
                                     IN_TILE, ATT_TILE)
        o = _fox_attention(qat, ka, vt, ATT_TILE)
        x2 = _post(x, u, o, kv, row2(norm_mix_g[l]), w_g, row2(b_gate[l]),
                   _pool_fold(pool_w[l], row2(pool_scale[l]), w_pool_out[l]),
                   w_fox_out[l].astype(BF16), w_out[l].astype(BF16), row2(norm_x_g[l]),
                   w_xq[l].astype(BF16), w_xo[l].astype(BF16), POST_TILE)
        x = _ffn(x2.reshape(B * S, D), row2(norm_ffn_g[l]), w_ffn_in[l].astype(BF16),
                 w_ffn_out[l].astype(BF16), row2(norm_final_g), FFN_TILE,
                 final_norm=(l == depth - 1)).reshape(B, S, D)
    return x
```
